```python
import jax, jax.numpy as jnp
from jax import lax
import numpy as np

D_MODEL = 1024
BATCH = 8
SEQ = 2048
DEPTH = 1
DEC_BATCH = 4
DEC_SEQ = 8192
PAST_LEN = 128

ATT_HEADS = 8
ATT_KV_HEADS = 2
ATT_HEAD_DIM = 64
WINDOW = 128
ATT_BLOCK = 128
ROPE_THETA = 10000.0
GLA_HEADS = 4
GLA_DK = 64
GLA_DV = 128
GLA_GATE_RANK = 16
GLA_TAU = 16.0
GLA_CHUNK = 64
N_EXPERTS = 32
TOP_K = 4
D_FF = 1024
SWIGLU_LIMIT = 7.0
SWIGLU_ALPHA = 1.702
MOE_BLOCK = 128
LN_EPS = 1e-5
DEEPNORM_ALPHA = (2 * DEPTH) ** 0.25
DEEPNORM_BETA = (8 * DEPTH) ** -0.25

ATT_Q_W = ATT_HEADS * ATT_HEAD_DIM
ATT_KV_W = ATT_KV_HEADS * ATT_HEAD_DIM
GLA_K_W = GLA_HEADS * GLA_DK
GLA_V_W = GLA_HEADS * GLA_DV
MIX_WIDTH = ATT_Q_W + GLA_V_W
IN_SPLITS = (ATT_Q_W, ATT_KV_W, ATT_KV_W, GLA_K_W, GLA_K_W, GLA_V_W, GLA_V_W, GLA_GATE_RANK, GLA_GATE_RANK)
IN_WIDTH = sum(IN_SPLITS)
IN_OFFSETS = tuple(int(o) for o in np.cumsum(IN_SPLITS)[:-1])

kernel_name = "hymba_swa_gla_moe_deepnorm_encoder"


def layer_norm(x, g, b):
    xf = x.astype(jnp.float32)
    xc = xf - jnp.mean(xf, -1, keepdims=True)
    var = jnp.mean(xc * xc, -1, keepdims=True)
    return (xc * lax.rsqrt(var + LN_EPS) * g + b).astype(x.dtype)


def rope(t):
    S, D = t.shape[1], t.shape[-1]
    half = D // 2
    inv_freq = ROPE_THETA ** (-jnp.arange(half, dtype=jnp.float32) * 2.0 / D)
    ang = jnp.arange(S, dtype=jnp.float32)[:, None] * inv_freq[None, :]
    cos = jnp.cos(ang)[None, :, None, :].astype(t.dtype)
    sin = jnp.sin(ang)[None, :, None, :].astype(t.dtype)
    t1, t2 = t[..., :half], t[..., half:]
    return jnp.concatenate([t1 * cos - t2 * sin, t2 * cos + t1 * sin], -1)


def windowed_sink_attention(q, k, v, sinks):
    B, S, H, D = q.shape
    KV = k.shape[2]
    G = H // KV
    R = ATT_BLOCK
    nb = S // R
    qb = q.reshape(B, nb, R, KV, G, D)
    pad = ((0, 0), (R, R), (0, 0), (0, 0))
    kp = jnp.pad(k, pad).reshape(B, nb + 2, R, KV, D)
    vp = jnp.pad(v, pad).reshape(B, nb + 2, R, KV, D)
    kb = jnp.concatenate([kp[:, :-2], kp[:, 1:-1], kp[:, 2:]], axis=2)
    vb = jnp.concatenate([vp[:, :-2], vp[:, 1:-1], vp[:, 2:]], axis=2)
    s = jnp.einsum('bnqhgd,bnjhd->bnhgqj', qb, kb).astype(jnp.float32) * (D ** -0.5)
    qi = jnp.arange(R)[:, None]
    kj = jnp.arange(3 * R)[None, :]
    rel = kj - R - qi
    key_pos = jnp.arange(nb)[:, None, None] * R - R + kj[None]
    valid = (jnp.abs(rel) <= WINDOW)[None] & (key_pos >= 0) & (key_pos < S)
    s = jnp.where(valid[None, :, None, None], s, -jnp.inf)
    sink = jnp.broadcast_to(sinks.astype(jnp.float32).reshape(KV, G)[None, None, :, :, None, None], s.shape[:-1] + (1,))
    p = jax.nn.softmax(jnp.concatenate([s, sink], -1), axis=-1)[..., :-1]
    o = jnp.einsum('bnhgqj,bnjhd->bnqhgd', p.astype(v.dtype), vb)
    return o.reshape(B, S, H * D)


def gla_chunked(q, k, v, log_g):
    B, S, H, DK = q.shape
    DV = v.shape[-1]
    C = GLA_CHUNK

    def to_chunks(t):
        return jnp.moveaxis(t.reshape(B, S // C, C, H, t.shape[-1]), 1, 0)

    bcum = jnp.cumsum(to_chunks(log_g.astype(jnp.float32)), axis=2)
    tril = jnp.tril(jnp.ones((C, C), dtype=bool))

    def step(state, inp):
        qc, kc, vc, bc = inp
        qf, kf, vf = qc.astype(jnp.float32), kc.astype(jnp.float32), vc.astype(jnp.float32)
        o_inter = jnp.einsum('bchk,bhkv->bchv', qf * jnp.exp(bc), state)
        diff = jnp.where(tril[None, :, :, None, None], bc[:, :, None] - bc[:, None, :], -jnp.inf)
        a = jnp.sum(qf[:, :, None] * kf[:, None, :] * jnp.exp(diff), axis=-1)
        o_intra = jnp.einsum('bijh,bjhv->bihv', a, vf)
        b_last = bc[:, -1]
        new_state = jnp.exp(b_last)[..., None] * state + jnp.einsum('bjhk,bjhv->bhkv', kf * jnp.exp(b_last[:, None] - bc), vf)
        return new_state, o_inter + o_intra

    init = jnp.zeros((B, H, DK, DV), jnp.float32)
    _, o = lax.scan(step, init, (to_chunks(q), to_chunks(k), to_chunks(v), bcum))
    return jnp.moveaxis(o, 0, 1).reshape(B, S, H, DV).astype(v.dtype)


def mixer(x, w_in, w_gate_f, b_gate_f, w_gate_b, b_gate_b, sinks, gla_norm_g, w_out):
    B, S, _ = x.shape
    h = x @ w_in
    aq, ak, av, gq, gk, gv, gr, zf, zb = jnp.split(h, IN_OFFSETS, axis=-1)
    aq = rope(aq.reshape(B, S, ATT_HEADS, ATT_HEAD_DIM))
    ak = rope(ak.reshape(B, S, ATT_KV_HEADS, ATT_HEAD_DIM))
    av = av.reshape(B, S, ATT_KV_HEADS, ATT_HEAD_DIM)
    att_out = windowed_sink_attention(aq, ak, av, sinks)
    gq = gq.reshape(B, S, GLA_HEADS, GLA_DK) * (GLA_DK ** -0.5)
    gk = gk.reshape(B, S, GLA_HEADS, GLA_DK)
    gv = gv.reshape(B, S, GLA_HEADS, GLA_DV)
    lg_f = (jax.nn.log_sigmoid((zf @ w_gate_f + b_gate_f).astype(jnp.float32)) / GLA_TAU).reshape(B, S, GLA_HEADS, GLA_DK)
    lg_b = (jax.nn.log_sigmoid((zb @ w_gate_b + b_gate_b).astype(jnp.float32)) / GLA_TAU).reshape(B, S, GLA_HEADS, GLA_DK)
    o_f = gla_chunked(gq, gk, gv, lg_f)
    o_b = jnp.flip(gla_chunked(jnp.flip(gq, 1), jnp.flip(gk, 1), jnp.flip(gv, 1), jnp.flip(lg_b, 1)), 1)
    o = (o_f + o_b).astype(jnp.float32)
    o = o * lax.rsqrt(jnp.mean(o * o, -1, keepdims=True) + LN_EPS) * gla_norm_g
    gr = gr.reshape(B, S, GLA_HEADS, GLA_DV)
    gla_out = (o.astype(x.dtype) * jax.nn.silu(gr)).reshape(B, S, GLA_V_W)
    return jnp.concatenate([att_out, gla_out], axis=-1) @ w_out


def moe_ffn(x, w_router, b_router, w_gate_up, b_gate_up, w_down, b_down):
    B, S, D = x.shape
    T = B * S
    E, R, F = N_EXPERTS, MOE_BLOCK, D_FF
    xt = x.reshape(T, D)
    logits = (xt @ w_router + b_router).astype(jnp.float32)
    top_v, top_i = lax.top_k(logits, TOP_K)
    gates = jax.nn.softmax(top_v, axis=-1)
    M = T * TOP_K
    e = top_i.reshape(M)
    tok = jnp.arange(M, dtype=jnp.int32) // TOP_K
    order = jnp.argsort(e)
    e_s, tok_s, w_s = e[order], tok[order], gates.reshape(M)[order]
    counts = jnp.zeros((E,), jnp.int32).at[e].add(1)
    padded = ((counts + R - 1) // R) * R
    ends_pad = jnp.cumsum(padded)
    starts_pad = ends_pad - padded
    starts = jnp.cumsum(counts) - counts
    dest = starts_pad[e_s] + (jnp.arange(M, dtype=jnp.int32) - starts[e_s])
    n_blocks = -(-M // R) + E
    P = n_blocks * R
    row_tok = jnp.full((P,), T, jnp.int32).at[dest].set(tok_s)
    row_w = jnp.zeros((P,), jnp.float32).at[dest].set(w_s)
    block_start = jnp.arange(n_blocks, dtype=jnp.int32) * R
    block_e = jnp.minimum(jnp.sum(block_start[:, None] >= ends_pad[None, :], axis=1), E - 1).astype(jnp.int32)
    x_pad = jnp.concatenate([xt, jnp.zeros((1, D), xt.dtype)], axis=0)
    xb = x_pad[row_tok].reshape(n_blocks, R, D)

    def expert_block(args):
        xblk, ei = args
        gu = xblk @ w_gate_up[ei] + b_gate_up[ei]
        gate = jnp.minimum(gu[:, :F], SWIGLU_LIMIT)
        up = jnp.clip(gu[:, F:], -SWIGLU_LIMIT, SWIGLU_LIMIT)
        act = (up + 1) * gate * jax.nn.sigmoid(gate * SWIGLU_ALPHA)
        return act @ w_down[ei] + b_down[ei]

    yb = lax.map(expert_block, (xb, block_e)).reshape(P, D)
    y = jnp.zeros((T + 1, D), jnp.float32).at[row_tok].add(yb.astype(jnp.float32) * row_w[:, None])
    return y[:T].astype(x.dtype).reshape(B, S, D)


def run_trunk(x, w_in, w_gate_f, b_gate_f, w_gate_b, b_gate_b, sinks, gla_norm_g, w_out, ln1_g, ln1_b,
              w_router, b_router, w_gate_up, b_gate_up, w_down, b_down, ln2_g, ln2_b):
    for l in range(DEPTH):
        m = mixer(x, w_in[l], w_gate_f[l], b_gate_f[l], w_gate_b[l], b_gate_b[l], sinks[l], gla_norm_g[l], w_out[l])
        x = layer_norm(DEEPNORM_ALPHA * x + m, ln1_g[l], ln1_b[l])
        f = moe_ffn(x, w_router[l], b_router[l], w_gate_up[l], b_gate_up[l], w_down[l], b_down[l])
        x = layer_norm(DEEPNORM_ALPHA * x + f, ln2_g[l], ln2_b[l])
    return x


def setup_inputs(seed: int = 0) -> dict:
    key = jax.random.key(seed)
    ks = jax.random.split(key, 24)
    f32 = jnp.float32
    nrm = lambda k, shape: jax.random.normal(k, shape, f32)
    L, D, E, F = DEPTH, D_MODEL, N_EXPERTS, D_FF
    return {
        'x_prompt': nrm(ks[0], (BATCH, SEQ, D)),
        'x_sample': nrm(ks[1], (DEC_BATCH, DEC_SEQ, D)),
        'w_in': nrm(ks[2], (L, D, IN_WIDTH)) * D ** -0.5,
        'w_gate_f': nrm(ks[3], (L, GLA_GATE_RANK, GLA_K_W)) * GLA_GATE_RANK ** -0.5,
        'b_gate_f': nrm(ks[4], (L, GLA_K_W)) * 0.1,
        'w_gate_b': nrm(ks[5], (L, GLA_GATE_RANK, GLA_K_W)) * GLA_GATE_RANK ** -0.5,
        'b_gate_b': nrm(ks[6], (L, GLA_K_W)) * 0.1,
        'sinks': nrm(ks[7], (L, ATT_HEADS)) * 0.5,
        'gla_norm_g': 1.0 + 0.02 * nrm(ks[8], (L, GLA_DV)),
        'w_out': nrm(ks[9], (L, MIX_WIDTH, D)) * (MIX_WIDTH ** -0.5 * DEEPNORM_BETA),
        'ln1_g': 1.0 + 0.02 * nrm(ks[10], (L, D)),
        'ln1_b': 0.02 * nrm(ks[11], (L, D)),
        'w_router': nrm(ks[12], (L, D, E)) * D ** -0.5,
        'b_router': nrm(ks[13], (L, E)) * 0.01,
        'w_gate_up': nrm(ks[14], (L, E, D, 2 * F)) * D ** -0.5,
        'b_gate_up': nrm(ks[15], (L, E, 2 * F)) * 0.02,
        'w_down': nrm(ks[16], (L, E, F, D)) * (F ** -0.5 * DEEPNORM_BETA),
        'b_down': nrm(ks[17], (L, E, D)) * 0.02,
        'ln2_g': 1.0 + 0.02 * nrm(ks[18], (L, D)),
        'ln2_b': 0.02 * nrm(ks[19], (L, D)),
    }


def reference(x_prompt, x_sample, w_in, w_gate_f, b_gate_f, w_gate_b, b_gate_b, sinks, gla_norm_g, w_out, ln1_g, ln1_b,
              w_router, b_router, w_gate_up, b_gate_up, w_down, b_down, ln2_g, ln2_b):
    y_prompt = run_trunk(x_prompt, w_in, w_gate_f, b_gate_f, w_gate_b, b_gate_b, sinks, gla_norm_g, w_out, ln1_g, ln1_b,
                         w_router, b_router, w_gate_up, b_gate_up, w_down, b_down, ln2_g, ln2_b)
    y_sample = run_trunk(x_sample, w_in, w_gate_f, b_gate_f, w_gate_b, b_gate_b, sinks, gla_norm_g, w_out, ln1_g, ln1_b,
                         w_router, b_router, w_gate_up, b_gate_up, w_down, b_down, ln2_g, ln2_b)
    return (y_prompt, y_sample)
```

```python
import functools

import jax
import jax.numpy as jnp
import numpy as np
from jax import lax
from jax.experimental import pallas as pl
from jax.experimental.pallas import tpu as pltpu

D_MODEL = 1024
DEPTH = 1
ATT_HEADS = 8
ATT_KV_HEADS = 2
ATT_HEAD_DIM = 64
ATT_GROUP = ATT_HEADS // ATT_KV_HEADS
WINDOW = 128
ATT_BLOCK = 128
ROPE_THETA = 10000.0
GLA_HEADS = 4
GLA_DK = 64
GLA_DV = 128
GLA_GATE_RANK = 16
GLA_TAU = 16.0
N_EXPERTS = 32
TOP_K = 4
D_FF = 1024
SWIGLU_LIMIT = 7.0
SWIGLU_ALPHA = 1.702
MOE_BLOCK = 128
LN_EPS = 1e-5
DEEPNORM_ALPHA = (2 * DEPTH) ** 0.25

ATT_Q_W = ATT_HEADS * ATT_HEAD_DIM
ATT_KV_W = ATT_KV_HEADS * ATT_HEAD_DIM
GLA_K_W = GLA_HEADS * GLA_DK
GLA_V_W = GLA_HEADS * GLA_DV
OFF_AQ = 0
OFF_AK = OFF_AQ + ATT_Q_W
OFF_AV = OFF_AK + ATT_KV_W
OFF_GQ = OFF_AV + ATT_KV_W
OFF_GK = OFF_GQ + GLA_K_W
OFF_GV = OFF_GK + GLA_K_W
OFF_GR = OFF_GV + GLA_V_W
OFF_Z = OFF_GR + GLA_V_W
IN_WIDTH = OFF_Z + 2 * GLA_GATE_RANK

LANES = 128
PROJ_TILE = 256
GLA_CHUNK = 128
GLA_FAST_MAX_DECAY = 60.0
COMBINE_TILE = 128
NEG_BIG = -1e30
VMEM_LIMIT = 48 * 1024 * 1024

BF16 = jnp.bfloat16
F32 = jnp.float32


def _dot(a, b):
    return jnp.dot(a, b, preferred_element_type=F32)


def _dot_nt(a, b):
    return lax.dot_general(a, b, (((1,), (1,)), ((), ())), preferred_element_type=F32)


def _dot_tn(a, b):
    return lax.dot_general(a, b, (((0,), (0,)), ((), ())), preferred_element_type=F32)


def _split_bf16(x):
    hi = x.astype(BF16)
    lo = (x - hi.astype(F32)).astype(BF16)
    return hi, lo


def _in_proj_kernel(x_ref, w_ref, wgf_ref, bgf_ref, wgb_ref, bgb_ref, cos_ref, sin_ref,
                    aq_ref, ak_ref, av_ref, gq_ref, gk_ref, gv_ref, gr_ref, lgf_ref, lgb_ref):
    xb = x_ref[...].astype(BF16)
    cos = cos_ref[...]
    sin = sin_ref[...]
    lane = lax.broadcasted_iota(jnp.int32, (x_ref.shape[0], LANES), 1)
    first_half = (lane % ATT_HEAD_DIM) < (ATT_HEAD_DIM // 2)

    def rope(t):
        swapped = jnp.where(first_half, pltpu.roll(t, LANES - ATT_HEAD_DIM // 2, 1),
                            pltpu.roll(t, ATT_HEAD_DIM // 2, 1))
        return t * cos + swapped * sin

    q_scale = ATT_HEAD_DIM ** -0.5
    for c in range(ATT_Q_W // LANES):
        t = _dot(xb, w_ref[:, OFF_AQ + c * LANES:OFF_AQ + (c + 1) * LANES])
        aq_ref[:, c * LANES:(c + 1) * LANES] = (rope(t) * q_scale).astype(aq_ref.dtype)
    ak_ref[...] = rope(_dot(xb, w_ref[:, OFF_AK:OFF_AK + ATT_KV_W])).astype(ak_ref.dtype)
    av_ref[...] = _dot(xb, w_ref[:, OFF_AV:OFF_AV + ATT_KV_W]).astype(av_ref.dtype)
    gq_ref[...] = _dot(xb, w_ref[:, OFF_GQ:OFF_GQ + GLA_K_W]) * (GLA_DK ** -0.5)
    gk_ref[...] = _dot(xb, w_ref[:, OFF_GK:OFF_GK + GLA_K_W])
    gv_ref[...] = _dot(xb, w_ref[:, OFF_GV:OFF_GV + GLA_V_W]).astype(gv_ref.dtype)
    gr_ref[...] = _dot(xb, w_ref[:, OFF_GR:OFF_GR + GLA_V_W]).astype(gr_ref.dtype)
    z = _dot(xb, w_ref[:, OFF_Z:OFF_Z + 2 * GLA_GATE_RANK])

    def log_decay(zr, wg_ref, bg_ref):
        pre = _dot(zr.astype(BF16), wg_ref[...]) + bg_ref[...]
        log_sig = jnp.minimum(pre, 0.0) - jnp.log(1.0 + jnp.exp(-jnp.abs(pre)))
        return log_sig * (1.0 / GLA_TAU)

    lgf_ref[...] = log_decay(z[:, :GLA_GATE_RANK], wgf_ref, bgf_ref)
    lgb_ref[...] = log_decay(z[:, GLA_GATE_RANK:], wgb_ref, bgb_ref)


def _in_proj(x2, w_in, wgf, bgf, wgb, bgb, cos_t, sin_t, seq):
    T = x2.shape[0]
    tm = PROJ_TILE
    assert T % tm == 0 and seq % tm == 0
    pos_tiles = seq // tm
    row = lambda w: pl.BlockSpec((tm, w), lambda i: (i, 0))
    full = lambda a: pl.BlockSpec(a.shape, lambda i: (0,) * a.ndim)
    pos = pl.BlockSpec((tm, LANES), lambda i: (i % pos_tiles, 0))
    out_w = [(ATT_Q_W, BF16), (ATT_KV_W, BF16), (ATT_KV_W, BF16), (GLA_K_W, F32), (GLA_K_W, F32),
             (GLA_V_W, BF16), (GLA_V_W, BF16), (GLA_K_W, F32), (GLA_K_W, F32)]
    return pl.pallas_call(
        _in_proj_kernel,
        grid=(T // tm,),
        in_specs=[row(D_MODEL), full(w_in), full(wgf), full(bgf), full(wgb), full(bgb), pos, pos],
        out_specs=[row(w) for w, _ in out_w],
        out_shape=[jax.ShapeDtypeStruct((T, w), dt) for w, dt in out_w],
        compiler_params=pltpu.CompilerParams(dimension_semantics=("arbitrary",), vmem_limit_bytes=VMEM_LIMIT),
        name="in_proj",
    )(x2, w_in, wgf, bgf, wgb, bgb, cos_t, sin_t)


def _attention_kernel(sinks_ref, q_ref, kp_ref, kc_ref, kn_ref, vp_ref, vc_ref, vn_ref, o_ref, *, seq):
    n = pl.program_id(1)
    R = ATT_BLOCK
    kcat = jnp.concatenate([kp_ref[...], kc_ref[...], kn_ref[...]], axis=0)
    vcat = jnp.concatenate([vp_ref[...], vc_ref[...], vn_ref[...]], axis=0)
    qi = lax.broadcasted_iota(jnp.int32, (R, 3 * R), 0)
    kj = lax.broadcasted_iota(jnp.int32, (R, 3 * R), 1)
    rel = kj - R - qi
    key_pos = n * R - R + kj
    valid = (jnp.abs(rel) <= WINDOW) & (key_pos >= 0) & (key_pos < seq)
    D = ATT_HEAD_DIM
    for kv in range(ATT_KV_HEADS):
        kh = kcat[:, kv * D:(kv + 1) * D]
        vh = vcat[:, kv * D:(kv + 1) * D]
        for g in range(ATT_GROUP):
            h = kv * ATT_GROUP + g
            s = _dot_nt(q_ref[:, h * D:(h + 1) * D], kh)
            s = jnp.where(valid, s, NEG_BIG)
            sink = sinks_ref[h]
            m = jnp.maximum(jnp.max(s, axis=1, keepdims=True), sink)
            p = jnp.exp(s - m)
            den = jnp.sum(p, axis=1, keepdims=True) + jnp.exp(sink - m)
            o = _dot(p.astype(BF16), vh) / den
            o_ref[:, h * D:(h + 1) * D] = o.astype(o_ref.dtype)


def _attention(aq, ak, av, sinks, batch, seq):
    R = ATT_BLOCK
    nb = seq // R
    qspec = pl.BlockSpec((R, ATT_Q_W), lambda b, n: (b * nb + n, 0))
    prev = pl.BlockSpec((R, ATT_KV_W), lambda b, n: (b * nb + jnp.maximum(n - 1, 0), 0))
    cur = pl.BlockSpec((R, ATT_KV_W), lambda b, n: (b * nb + n, 0))
    nxt = pl.BlockSpec((R, ATT_KV_W), lambda b, n: (b * nb + jnp.minimum(n + 1, nb - 1), 0))
    return pl.pallas_call(
        functools.partial(_attention_kernel, seq=seq),
        grid=(batch, nb),
        in_specs=[pl.BlockSpec(memory_space=pltpu.SMEM), qspec, prev, cur, nxt, prev, cur, nxt],
        out_specs=qspec,
        out_shape=jax.ShapeDtypeStruct(aq.shape, BF16),
        compiler_params=pltpu.CompilerParams(dimension_semantics=("arbitrary", "arbitrary"),
                                             vmem_limit_bytes=VMEM_LIMIT),
        name="swa_attention",
    )(sinks, aq, ak, ak, ak, av, av, av)


def _gla_direction(q_ref, k_ref, v_ref, lg_ref, o_ref, st_ref, oi_ref, bc_ref, vf_ref, *, reverse):
    C = q_ref.shape[0]
    ri = lax.broadcasted_iota(jnp.int32, (C, C), 0)
    cj = lax.broadcasted_iota(jnp.int32, (C, C), 1)
    pair = (cj >= ri) if reverse else (cj <= ri)
    tri = pair.astype(BF16)
    lg_hi, lg_lo = _split_bf16(lg_ref[...])
    bc = _dot(tri, lg_hi) + _dot(tri, lg_lo)
    end = 0 if reverse else C - 1
    b_end = bc[end:end + 1, :]
    q = q_ref[...]
    k = k_ref[...]
    qt = (q * jnp.exp(bc)).astype(BF16)
    k_to_end = (k * jnp.exp(b_end - bc)).astype(BF16)
    fast_ok = jnp.max(-b_end) <= GLA_FAST_MAX_DECAY

    @pl.when(fast_ok)
    def _():
        kt = (k * jnp.exp(-bc)).astype(BF16)
        for h in range(GLA_HEADS):
            ks = slice(h * GLA_DK, (h + 1) * GLA_DK)
            vs = slice(h * GLA_DV, (h + 1) * GLA_DV)
            a = jnp.where(pair, _dot_nt(qt[:, ks], kt[:, ks]), 0.0)
            oi_ref[:, vs] = _dot(a.astype(BF16), v_ref[:, vs])

    @pl.when(jnp.logical_not(fast_ok))
    def _():
        bc_ref[...] = bc
        vf_ref[...] = v_ref[...].astype(F32)
        oi_ref[...] = jnp.zeros_like(oi_ref)
        rows = lax.broadcasted_iota(jnp.int32, (C, 1), 0)

        def body(j, carry):
            kj = k_ref[pl.ds(j, 1), :]
            bj = bc_ref[pl.ds(j, 1), :]
            vj = vf_ref[pl.ds(j, 1), :]
            live = (rows <= j) if reverse else (rows >= j)
            t = jnp.where(live, q * kj * jnp.exp(jnp.minimum(bc - bj, 0.0)), 0.0)
            for h in range(GLA_HEADS):
                a = jnp.sum(t[:, h * GLA_DK:(h + 1) * GLA_DK], axis=1, keepdims=True)
                vs = slice(h * GLA_DV, (h + 1) * GLA_DV)
                oi_ref[:, vs] += a * vj[:, vs]
            return carry

        lax.fori_loop(0, C, body, 0)

    st = st_ref[...]
    st_b = st.astype(BF16)
    new_cols = []
    for h in range(GLA_HEADS):
        ks = slice(h * GLA_DK, (h + 1) * GLA_DK)
        vs = slice(h * GLA_DV, (h + 1) * GLA_DV)
        o_ref[:, vs] = _dot_nt(qt[:, ks], st_b[:, ks]) + oi_ref[:, vs]
        new_cols.append(_dot_tn(v_ref[:, vs], k_to_end[:, ks]))
    st_ref[...] = st * jnp.exp(b_end) + jnp.concatenate(new_cols, axis=1)


def _gla_kernel(qf_ref, kf_ref, vf_ref, lgf_ref, qb_ref, kb_ref, vb_ref, lgb_ref, of_ref, ob_ref,
                stf_ref, stb_ref, oi_ref, bc_ref, vv_ref):
    @pl.when(pl.program_id(1) == 0)
    def _():
        stf_ref[...] = jnp.zeros_like(stf_ref)
        stb_ref[...] = jnp.zeros_like(stb_ref)

    _gla_direction(qf_ref, kf_ref, vf_ref, lgf_ref, of_ref, stf_ref, oi_ref, bc_ref, vv_ref, reverse=False)
    _gla_direction(qb_ref, kb_ref, vb_ref, lgb_ref, ob_ref, stb_ref, oi_ref, bc_ref, vv_ref, reverse=True)


def _gla(gq, gk, gv, lgf, lgb, batch, seq):
    C = GLA_CHUNK
    nc = seq // C
    T = gq.shape[0]
    fwd = lambda w: pl.BlockSpec((C, w), lambda b, n: (b * nc + n, 0))
    bwd = lambda w: pl.BlockSpec((C, w), lambda b, n: (b * nc + nc - 1 - n, 0))
    return pl.pallas_call(
        _gla_kernel,
        grid=(batch, nc),
        in_specs=[fwd(GLA_K_W), fwd(GLA_K_W), fwd(GLA_V_W), fwd(GLA_K_W),
                  bwd(GLA_K_W), bwd(GLA_K_W), bwd(GLA_V_W), bwd(GLA_K_W)],
        out_specs=[fwd(GLA_V_W), bwd(GLA_V_W)],
        out_shape=[jax.ShapeDtypeStruct((T, GLA_V_W), F32)] * 2,
        scratch_shapes=[pltpu.VMEM((GLA_DV, GLA_K_W), F32), pltpu.VMEM((GLA_DV, GLA_K_W), F32),
                        pltpu.VMEM((C, GLA_V_W), F32), pltpu.VMEM((C, GLA_K_W), F32),
                        pltpu.VMEM((C, GLA_V_W), F32)],
        compiler_params=pltpu.CompilerParams(dimension_semantics=("arbitrary", "arbitrary"),
                                             vmem_limit_bytes=VMEM_LIMIT),
        name="gla_bidir",
    )(gq, gk, gv, lgf, gq, gk, gv, lgb)


def _layer_norm(y, g, b):
    yc = y - jnp.mean(y, axis=-1, keepdims=True)
    var = jnp.mean(yc * yc, axis=-1, keepdims=True)
    return yc * lax.rsqrt(var + LN_EPS) * g + b


def _post_mixer_kernel(of_ref, ob_ref, gr_ref, att_ref, x_ref, gng_ref, wo_ref, g1_ref, b1_ref,
                       wrh_ref, wrl_ref, br_ref, x1_ref, ei_ref, gw_ref):
    tm = x_ref.shape[0]
    m = _dot(att_ref[...], wo_ref[0:ATT_Q_W, :])
    for h in range(GLA_HEADS):
        vs = slice(h * GLA_DV, (h + 1) * GLA_DV)
        o = of_ref[:, vs] + ob_ref[:, vs]
        o = o * lax.rsqrt(jnp.mean(o * o, axis=-1, keepdims=True) + LN_EPS) * gng_ref[...]
        r = gr_ref[:, vs].astype(F32)
        gated = o * (r / (1.0 + jnp.exp(-r)))
        m = m + _dot(gated.astype(BF16), wo_ref[ATT_Q_W + h * GLA_DV:ATT_Q_W + (h + 1) * GLA_DV, :])
    x1 = _layer_norm(DEEPNORM_ALPHA * x_ref[...] + m, g1_ref[...], b1_ref[...])
    x1_ref[...] = x1

    xh, xl = _split_bf16(x1)
    logits = _dot(xh, wrh_ref[...]) + _dot(xl, wrh_ref[...]) + _dot(xh, wrl_ref[...]) + br_ref[...]
    lane = lax.broadcasted_iota(jnp.int32, (tm, N_EXPERTS), 1).astype(F32)
    out_lane = lax.broadcasted_iota(jnp.int32, (tm, LANES), 1)
    vals = logits
    top_v = []
    e_out = jnp.zeros((tm, LANES), jnp.int32)
    for kk in range(TOP_K):
        mx = jnp.max(vals, axis=1, keepdims=True)
        idx = jnp.min(jnp.where(vals == mx, lane, float(N_EXPERTS)), axis=1, keepdims=True)
        top_v.append(mx)
        e_out = jnp.where(out_lane == kk, idx.astype(jnp.int32), e_out)
        vals = jnp.where(lane == idx, -jnp.inf, vals)
    ex = [jnp.exp(v - top_v[0]) for v in top_v]
    den = ex[0] + ex[1] + ex[2] + ex[3]
    g_out = jnp.zeros((tm, LANES), F32)
    for kk in range(TOP_K):
        g_out = jnp.where(out_lane == kk, ex[kk] / den, g_out)
    ei_ref[...] = e_out
    gw_ref[...] = g_out


def _post_mixer(o_f, o_b, gr, att, x2, gng, w_out, g1, b1, wr_hi, wr_lo, br):
    T = x2.shape[0]
    tm = PROJ_TILE
    row = lambda w: pl.BlockSpec((tm, w), lambda i: (i, 0))
    full = lambda a: pl.BlockSpec(a.shape, lambda i: (0,) * a.ndim)
    return pl.pallas_call(
        _post_mixer_kernel,
        grid=(T // tm,),
        in_specs=[row(GLA_V_W), row(GLA_V_W), row(GLA_V_W), row(ATT_Q_W), row(D_MODEL), full(gng), full(w_out),
                  full(g1), full(b1), full(wr_hi), full(wr_lo), full(br)],
        out_specs=[row(D_MODEL), row(LANES), row(LANES)],
        out_shape=[jax.ShapeDtypeStruct((T, D_MODEL), F32), jax.ShapeDtypeStruct((T, LANES), jnp.int32),
                   jax.ShapeDtypeStruct((T, LANES), F32)],
        compiler_params=pltpu.CompilerParams(dimension_semantics=("arbitrary",), vmem_limit_bytes=VMEM_LIMIT),
        name="post_mixer_router",
    )(o_f, o_b, gr, att, x2, gng, w_out, g1, b1, wr_hi, wr_lo, br)


def _row_copy(src_hbm, dst_buf, sem, src_row, dst_row):
    return pltpu.make_async_copy(src_hbm.at[pl.ds(src_row, 1), :], dst_buf.at[pl.ds(dst_row, 1), :], sem)


def _moe_kernel(be_ref, idx_ref, x_hbm, wgu_ref, bgu_ref, wd_ref, bd_ref, y_ref, xbuf, sem):
    del be_ref
    R = MOE_BLOCK

    def issue(r, carry):
        _row_copy(x_hbm, xbuf, sem, idx_ref[0, 0, r], r).start()
        return carry

    lax.fori_loop(0, R, issue, 0)
    pltpu.make_async_copy(x_hbm.at[pl.ds(0, R), :], xbuf, sem).wait()
    xb = xbuf[...].astype(BF16)
    gu = _dot(xb, wgu_ref[0]) + bgu_ref[0]
    gate = jnp.minimum(gu[:, :D_FF], SWIGLU_LIMIT)
    up = jnp.clip(gu[:, D_FF:], -SWIGLU_LIMIT, SWIGLU_LIMIT)
    act = (up + 1.0) * gate / (1.0 + jnp.exp(-SWIGLU_ALPHA * gate))
    y_ref[...] = _dot(act.astype(BF16), wd_ref[0]) + bd_ref[0]


def _moe(x1, block_e, row_tok, w_gu, b_gu, w_dn, b_dn):
    R = MOE_BLOCK
    n_blocks = block_e.shape[0]
    grid_spec = pltpu.PrefetchScalarGridSpec(
        num_scalar_prefetch=1,
        grid=(n_blocks,),
        in_specs=[
            pl.BlockSpec((1, 1, R), lambda i, be: (i, 0, 0), memory_space=pltpu.SMEM),
            pl.BlockSpec(memory_space=pl.ANY),
            pl.BlockSpec((1, D_MODEL, 2 * D_FF), lambda i, be: (be[i], 0, 0)),
            pl.BlockSpec((1, 1, 2 * D_FF), lambda i, be: (be[i], 0, 0)),
            pl.BlockSpec((1, D_FF, D_MODEL), lambda i, be: (be[i], 0, 0)),
            pl.BlockSpec((1, 1, D_MODEL), lambda i, be: (be[i], 0, 0)),
        ],
        out_specs=pl.BlockSpec((R, D_MODEL), lambda i, be: (i, 0)),
        scratch_shapes=[pltpu.VMEM((R, D_MODEL), F32), pltpu.SemaphoreType.DMA(())],
    )
    return pl.pallas_call(
        _moe_kernel,
        grid_spec=grid_spec,
        out_shape=jax.ShapeDtypeStruct((n_blocks * R, D_MODEL), F32),
        compiler_params=pltpu.CompilerParams(dimension_semantics=("arbitrary",), vmem_limit_bytes=VMEM_LIMIT),
        name="moe_experts",
    )(block_e, row_tok.reshape(n_blocks, 1, R), x1, w_gu, b_gu, w_dn, b_dn)


def _combine_kernel(dest_ref, yb_hbm, gw_ref, x1_ref, g2_ref, b2_ref, out_ref, ybuf, sem):
    tm = x1_ref.shape[0]

    def issue(t, carry):
        for kk in range(TOP_K):
            _row_copy(yb_hbm, ybuf.at[kk], sem, dest_ref[0, 0, t * TOP_K + kk], t).start()
        return carry

    lax.fori_loop(0, tm, issue, 0)
    for kk in range(TOP_K):
        pltpu.make_async_copy(yb_hbm.at[pl.ds(0, tm), :], ybuf.at[kk], sem).wait()
    gw = gw_ref[...]
    f = gw[:, 0:1] * ybuf[0]
    for kk in range(1, TOP_K):
        f = f + gw[:, kk:kk + 1] * ybuf[kk]
    out_ref[...] = _layer_norm(DEEPNORM_ALPHA * x1_ref[...] + f, g2_ref[...], b2_ref[...])


def _combine(dest, yb, gw, x1, g2, b2):
    T = x1.shape[0]
    tm = COMBINE_TILE
    row = lambda w: pl.BlockSpec((tm, w), lambda i: (i, 0))
    full = lambda a: pl.BlockSpec(a.shape, lambda i: (0,) * a.ndim)
    return pl.pallas_call(
        _combine_kernel,
        grid=(T // tm,),
        in_specs=[pl.BlockSpec((1, 1, tm * TOP_K), lambda i: (i, 0, 0), memory_space=pltpu.SMEM),
                  pl.BlockSpec(memory_space=pl.ANY), row(LANES), row(D_MODEL), full(g2), full(b2)],
        out_specs=row(D_MODEL),
        out_shape=jax.ShapeDtypeStruct((T, D_MODEL), F32),
        scratch_shapes=[pltpu.VMEM((TOP_K, tm, D_MODEL), F32), pltpu.SemaphoreType.DMA(())],
        compiler_params=pltpu.CompilerParams(dimension_semantics=("arbitrary",), vmem_limit_bytes=VMEM_LIMIT),
        name="moe_combine_ln",
    )(dest.reshape(T // tm, 1, tm * TOP_K), yb, gw, x1, g2, b2)


def _routing_tables(e_idx):
    T = e_idx.shape[0]
    R, E = MOE_BLOCK, N_EXPERTS
    M = T * TOP_K
    multi_hot = jnp.sum((e_idx[:, :, None] == jnp.arange(E, dtype=jnp.int32)).astype(jnp.int32), axis=1)
    before = jnp.cumsum(multi_hot, axis=0) - multi_hot
    rank = jnp.take_along_axis(before, e_idx, axis=1)
    counts = before[-1] + multi_hot[-1]
    padded = ((counts + R - 1) // R) * R
    ends_pad = jnp.cumsum(padded)
    starts_pad = ends_pad - padded
    dest = starts_pad[e_idx] + rank
    n_blocks = M // R + E
    tok = jnp.broadcast_to(jnp.arange(T, dtype=jnp.int32)[:, None], (T, TOP_K))
    row_tok = jnp.zeros((n_blocks * R,), jnp.int32).at[dest.reshape(-1)].set(tok.reshape(-1))
    block_start = jnp.arange(n_blocks, dtype=jnp.int32) * R
    block_e = jnp.minimum(jnp.sum(block_start[:, None] >= ends_pad[None, :], axis=1), E - 1).astype(jnp.int32)
    return dest.astype(jnp.int32), row_tok, block_e


def _rope_tables(seq):
    half = ATT_HEAD_DIM // 2
    inv_freq = ROPE_THETA ** (-jnp.arange(half, dtype=F32) * 2.0 / ATT_HEAD_DIM)
    ang = jnp.arange(seq, dtype=F32)[:, None] * inv_freq[None, :]
    cos, sin = jnp.cos(ang), jnp.sin(ang)
    reps = LANES // ATT_HEAD_DIM
    cos_t = jnp.tile(jnp.concatenate([cos, cos], axis=1), (1, reps))
    sin_t = jnp.tile(jnp.concatenate([-sin, sin], axis=1), (1, reps))
    return cos_t, sin_t


def _run_trunk(x, p):
    B, S, D = x.shape
    x2 = x.reshape(B * S, D)
    cos_t, sin_t = _rope_tables(S)
    aq, ak, av, gq, gk, gv, gr, lgf, lgb = _in_proj(x2, p["w_in"], p["wgf"], p["bgf"], p["wgb"], p["bgb"],
                                                    cos_t, sin_t, S)
    att = _attention(aq, ak, av, p["sinks"], B, S)
    o_f, o_b = _gla(gq, gk, gv, lgf, lgb, B, S)
    x1, e_pad, g_pad = _post_mixer(o_f, o_b, gr, att, x2, p["gng"], p["w_out"], p["g1"], p["b1"],
                                   p["wr_hi"], p["wr_lo"], p["br"])
    dest, row_tok, block_e = _routing_tables(e_pad[:, :TOP_K])
    yb = _moe(x1, block_e, row_tok, p["w_gu"], p["b_gu"], p["w_dn"], p["b_dn"])
    y = _combine(dest, yb, g_pad, x1, p["g2"], p["b2"])
    return y.reshape(B, S, D)


def kernel(x_prompt, x_sample, w_in, w_gate_f, b_gate_f, w_gate_b, b_gate_b, sinks, gla_norm_g, w_out, ln1_g, ln1_b,
           w_router, b_router, w_gate_up, b_gate_up, w_down, b_down, ln2_g, ln2_b):
    l = 0
    wr = w_router[l]
    wr_hi = wr.astype(BF16)
    p = dict(
        w_in=w_in[l].astype(BF16),
        wgf=w_gate_f[l].astype(BF16), bgf=b_gate_f[l].reshape(1, -1),
        wgb=w_gate_b[l].astype(BF16), bgb=b_gate_b[l].reshape(1, -1),
        sinks=sinks[l], gng=gla_norm_g[l].reshape(1, -1), w_out=w_out[l].astype(BF16),
        g1=ln1_g[l].reshape(1, -1), b1=ln1_b[l].reshape(1, -1),
        wr_hi=wr_hi, wr_lo=(wr - wr_hi.astype(F32)).astype(BF16), br=b_router[l].reshape(1, -1),
        w_gu=w_gate_up[l].astype(BF16), b_gu=b_gate_up[l].reshape(N_EXPERTS, 1, -1),
        w_dn=w_down[l].astype(BF16), b_dn=b_down[l].reshape(N_EXPERTS, 1, -1),
        g2=ln2_g[l].reshape(1, -1), b2=ln2_b[l].reshape(1, -1),
    )
    return (_run_trunk(x_prompt, p), _run_trunk(x_sample, p))
```

```python
import functools

import jax
import jax.numpy as jnp
import numpy as np
from jax import lax
from jax.experimental import pallas as pl
from jax.experimental.pallas import tpu as pltpu

D_MODEL = 1024
DEPTH = 1
ATT_HEADS = 8
ATT_KV_HEADS = 2
ATT_HEAD_DIM = 64
ATT_GROUP = ATT_HEADS // ATT_KV_HEADS
WINDOW = 128
ATT_BLOCK = 128
ROPE_THETA = 10000.0
GLA_HEADS = 4
GLA_DK = 64
GLA_DV = 128
GLA_GATE_RANK = 16
GLA_TAU = 16.0
N_EXPERTS = 32
TOP_K = 4
D_FF = 1024
SWIGLU_LIMIT = 7.0
SWIGLU_ALPHA = 1.702
MOE_BLOCK = 128
LN_EPS = 1e-5
DEEPNORM_ALPHA = (2 * DEPTH) ** 0.25

ATT_Q_W = ATT_HEADS * ATT_HEAD_DIM
ATT_KV_W = ATT_KV_HEADS * ATT_HEAD_DIM
GLA_K_W = GLA_HEADS * GLA_DK
GLA_V_W = GLA_HEADS * GLA_DV
OFF_AQ = 0
OFF_AK = OFF_AQ + ATT_Q_W
OFF_AV = OFF_AK + ATT_KV_W
OFF_GQ = OFF_AV + ATT_KV_W
OFF_GK = OFF_GQ + GLA_K_W
OFF_GV = OFF_GK + GLA_K_W
OFF_GR = OFF_GV + GLA_V_W
OFF_Z = OFF_GR + GLA_V_W
IN_WIDTH = OFF_Z + 2 * GLA_GATE_RANK

LANES = 128
PROJ_TILE = 256
GLA_CHUNK = 128
GLA_FAST_MAX_DECAY = 60.0
COMBINE_TILE = 256
NEG_BIG = -1e30
VMEM_LIMIT = 48 * 1024 * 1024

BF16 = jnp.bfloat16
F32 = jnp.float32


def _dot(a, b):
    return jnp.dot(a, b, preferred_element_type=F32)


def _dot_nt(a, b):
    return lax.dot_general(a, b, (((1,), (1,)), ((), ())), preferred_element_type=F32)


def _dot_tn(a, b):
    return lax.dot_general(a, b, (((0,), (0,)), ((), ())), preferred_element_type=F32)


def _split_bf16(x):
    hi = x.astype(BF16)
    lo = (x - hi.astype(F32)).astype(BF16)
    return hi, lo


TOKEN_TILE_ROWS = D_MODEL // LANES


def _token_tile_spec(tokens, index_map=lambda i: (i, 0)):
    return pl.BlockSpec((tokens * TOKEN_TILE_ROWS, LANES), index_map)


def _load_token_tiles(ref, tokens):
    return jnp.concatenate([ref[pl.ds(c, tokens, stride=TOKEN_TILE_ROWS), :] for c in range(TOKEN_TILE_ROWS)],
                           axis=1)


def _store_token_tiles(ref, x):
    for c in range(TOKEN_TILE_ROWS):
        ref[pl.ds(c, x.shape[0], stride=TOKEN_TILE_ROWS), :] = x[:, c * LANES:(c + 1) * LANES]


def _in_proj_kernel(x_ref, w_ref, wgf_ref, bgf_ref, wgb_ref, bgb_ref, cos_ref, sin_ref,
                    aq_ref, ak_ref, av_ref, gq_ref, gk_ref, gv_ref, gr_ref, lgf_ref, lgb_ref):
    xb = x_ref[...].astype(BF16)
    cos = cos_ref[...]
    sin = sin_ref[...]
    lane = lax.broadcasted_iota(jnp.int32, (x_ref.shape[0], LANES), 1)
    first_half = (lane % ATT_HEAD_DIM) < (ATT_HEAD_DIM // 2)

    def rope(t):
        swapped = jnp.where(first_half, pltpu.roll(t, LANES - ATT_HEAD_DIM // 2, 1),
                            pltpu.roll(t, ATT_HEAD_DIM // 2, 1))
        return t * cos + swapped * sin

    q_scale = ATT_HEAD_DIM ** -0.5
    for c in range(ATT_Q_W // LANES):
        t = _dot(xb, w_ref[:, OFF_AQ + c * LANES:OFF_AQ + (c + 1) * LANES])
        aq_ref[:, c * LANES:(c + 1) * LANES] = (rope(t) * q_scale).astype(aq_ref.dtype)
    ak_ref[...] = rope(_dot(xb, w_ref[:, OFF_AK:OFF_AK + ATT_KV_W])).astype(ak_ref.dtype)
    av_ref[...] = _dot(xb, w_ref[:, OFF_AV:OFF_AV + ATT_KV_W]).astype(av_ref.dtype)
    gq_ref[...] = _dot(xb, w_ref[:, OFF_GQ:OFF_GQ + GLA_K_W]) * (GLA_DK ** -0.5)
    gk_ref[...] = _dot(xb, w_ref[:, OFF_GK:OFF_GK + GLA_K_W])
    gv_ref[...] = _dot(xb, w_ref[:, OFF_GV:OFF_GV + GLA_V_W]).astype(gv_ref.dtype)
    gr_ref[...] = _dot(xb, w_ref[:, OFF_GR:OFF_GR + GLA_V_W]).astype(gr_ref.dtype)
    z = _dot(xb, w_ref[:, OFF_Z:OFF_Z + 2 * GLA_GATE_RANK])

    def log_decay(zr, wg_ref, bg_ref):
        pre = _dot(zr.astype(BF16), wg_ref[...]) + bg_ref[...]
        log_sig = jnp.minimum(pre, 0.0) - jnp.log(1.0 + jnp.exp(-jnp.abs(pre)))
        return log_sig * (1.0 / GLA_TAU)

    lgf_ref[...] = log_decay(z[:, :GLA_GATE_RANK], wgf_ref, bgf_ref)
    lgb_ref[...] = log_decay(z[:, GLA_GATE_RANK:], wgb_ref, bgb_ref)


def _in_proj(x2, w_in, wgf, bgf, wgb, bgb, cos_t, sin_t, seq):
    T = x2.shape[0]
    tm = PROJ_TILE
    assert T % tm == 0 and seq % tm == 0
    pos_tiles = seq // tm
    row = lambda w: pl.BlockSpec((tm, w), lambda i: (i, 0))
    full = lambda a: pl.BlockSpec(a.shape, lambda i: (0,) * a.ndim)
    pos = pl.BlockSpec((tm, LANES), lambda i: (i % pos_tiles, 0))
    out_w = [(ATT_Q_W, BF16), (ATT_KV_W, BF16), (ATT_KV_W, BF16), (GLA_K_W, F32), (GLA_K_W, F32),
             (GLA_V_W, BF16), (GLA_V_W, BF16), (GLA_K_W, F32), (GLA_K_W, F32)]
    return pl.pallas_call(
        _in_proj_kernel,
        grid=(T // tm,),
        in_specs=[row(D_MODEL), full(w_in), full(wgf), full(bgf), full(wgb), full(bgb), pos, pos],
        out_specs=[row(w) for w, _ in out_w],
        out_shape=[jax.ShapeDtypeStruct((T, w), dt) for w, dt in out_w],
        compiler_params=pltpu.CompilerParams(dimension_semantics=("arbitrary",), vmem_limit_bytes=VMEM_LIMIT),
        name="in_proj",
    )(x2, w_in, wgf, bgf, wgb, bgb, cos_t, sin_t)


def _attention_kernel(sinks_ref, q_ref, kp_ref, kc_ref, kn_ref, vp_ref, vc_ref, vn_ref, o_ref, *, seq):
    n = pl.program_id(1)
    R = ATT_BLOCK
    kcat = jnp.concatenate([kp_ref[...], kc_ref[...], kn_ref[...]], axis=0)
    vcat = jnp.concatenate([vp_ref[...], vc_ref[...], vn_ref[...]], axis=0)
    qi = lax.broadcasted_iota(jnp.int32, (R, 3 * R), 0)
    kj = lax.broadcasted_iota(jnp.int32, (R, 3 * R), 1)
    rel = kj - R - qi
    key_pos = n * R - R + kj
    valid = (jnp.abs(rel) <= WINDOW) & (key_pos >= 0) & (key_pos < seq)
    D = ATT_HEAD_DIM
    for kv in range(ATT_KV_HEADS):
        kh = kcat[:, kv * D:(kv + 1) * D]
        vh = vcat[:, kv * D:(kv + 1) * D]
        for g in range(ATT_GROUP):
            h = kv * ATT_GROUP + g
            s = _dot_nt(q_ref[:, h * D:(h + 1) * D], kh)
            s = jnp.where(valid, s, NEG_BIG)
            sink = sinks_ref[h]
            m = jnp.maximum(jnp.max(s, axis=1, keepdims=True), sink)
            p = jnp.exp(s - m)
            den = jnp.sum(p, axis=1, keepdims=True) + jnp.exp(sink - m)
            o = _dot(p.astype(BF16), vh) / den
            o_ref[:, h * D:(h + 1) * D] = o.astype(o_ref.dtype)


def _attention(aq, ak, av, sinks, batch, seq):
    R = ATT_BLOCK
    nb = seq // R
    qspec = pl.BlockSpec((R, ATT_Q_W), lambda b, n: (b * nb + n, 0))
    prev = pl.BlockSpec((R, ATT_KV_W), lambda b, n: (b * nb + jnp.maximum(n - 1, 0), 0))
    cur = pl.BlockSpec((R, ATT_KV_W), lambda b, n: (b * nb + n, 0))
    nxt = pl.BlockSpec((R, ATT_KV_W), lambda b, n: (b * nb + jnp.minimum(n + 1, nb - 1), 0))
    return pl.pallas_call(
        functools.partial(_attention_kernel, seq=seq),
        grid=(batch, nb),
        in_specs=[pl.BlockSpec(memory_space=pltpu.SMEM), qspec, prev, cur, nxt, prev, cur, nxt],
        out_specs=qspec,
        out_shape=jax.ShapeDtypeStruct(aq.shape, BF16),
        compiler_params=pltpu.CompilerParams(dimension_semantics=("arbitrary", "arbitrary"),
                                             vmem_limit_bytes=VMEM_LIMIT),
        name="swa_attention",
    )(sinks, aq, ak, ak, ak, av, av, av)


def _gla_direction(q_ref, k_ref, v_ref, lg_ref, o_ref, st_ref, slow_scratch, *, reverse, fast):
    C = q_ref.shape[0]
    ri = lax.broadcasted_iota(jnp.int32, (C, C), 0)
    cj = lax.broadcasted_iota(jnp.int32, (C, C), 1)
    pair = (cj >= ri) if reverse else (cj <= ri)
    tri = pair.astype(BF16)
    lg_hi, lg_lo = _split_bf16(lg_ref[...])
    bc = _dot(tri, lg_hi) + _dot(tri, lg_lo)
    end = 0 if reverse else C - 1
    b_end = bc[end:end + 1, :]
    q = q_ref[...]
    k = k_ref[...]
    qt = (q * jnp.exp(bc)).astype(BF16)
    k_to_end = (k * jnp.exp(b_end - bc)).astype(BF16)
    heads = [(slice(h * GLA_DK, (h + 1) * GLA_DK), slice(h * GLA_DV, (h + 1) * GLA_DV)) for h in range(GLA_HEADS)]

    if fast:
        kt = (k * jnp.exp(-bc)).astype(BF16)
        intra = []
        for ks, vs in heads:
            a = jnp.where(pair, _dot_nt(qt[:, ks], kt[:, ks]), 0.0)
            intra.append(_dot(a.astype(BF16), v_ref[:, vs]))
    else:
        oi_ref, bc_ref, vf_ref = slow_scratch
        bc_ref[...] = bc
        vf_ref[...] = v_ref[...].astype(F32)
        oi_ref[...] = jnp.zeros_like(oi_ref)
        rows = lax.broadcasted_iota(jnp.int32, (C, 1), 0)

        def body(j, carry):
            kj = k_ref[pl.ds(j, 1), :]
            bj = bc_ref[pl.ds(j, 1), :]
            vj = vf_ref[pl.ds(j, 1), :]
            live = (rows <= j) if reverse else (rows >= j)
            t = jnp.where(live, q * kj * jnp.exp(jnp.minimum(bc - bj, 0.0)), 0.0)
            for ks, vs in heads:
                a = jnp.sum(t[:, ks], axis=1, keepdims=True)
                oi_ref[:, vs] += a * vj[:, vs]
            return carry

        lax.fori_loop(0, C, body, 0)
        intra = [oi_ref[:, vs] for _, vs in heads]

    st = st_ref[...]
    st_b = st.astype(BF16)
    new_cols = []
    for (ks, vs), o_intra in zip(heads, intra):
        o_ref[:, vs] = _dot_nt(qt[:, ks], st_b[:, ks]) + o_intra
        new_cols.append(_dot_tn(v_ref[:, vs], k_to_end[:, ks]))
    st_ref[...] = st * jnp.exp(b_end) + jnp.concatenate(new_cols, axis=1)


def _gla_kernel(qf_ref, kf_ref, vf_ref, lgf_ref, qb_ref, kb_ref, vb_ref, lgb_ref, of_ref, ob_ref,
                stf_ref, stb_ref, oi_ref, bc_ref, vv_ref):
    @pl.when(pl.program_id(1) == 0)
    def _():
        stf_ref[...] = jnp.zeros_like(stf_ref)
        stb_ref[...] = jnp.zeros_like(stb_ref)

    def both(fast):
        scratch = (oi_ref, bc_ref, vv_ref)
        _gla_direction(qf_ref, kf_ref, vf_ref, lgf_ref, of_ref, stf_ref, scratch, reverse=False, fast=fast)
        _gla_direction(qb_ref, kb_ref, vb_ref, lgb_ref, ob_ref, stb_ref, scratch, reverse=True, fast=fast)

    chunk_decay = jnp.minimum(jnp.min(jnp.sum(lgf_ref[...], axis=0, keepdims=True)),
                              jnp.min(jnp.sum(lgb_ref[...], axis=0, keepdims=True)))
    fast_ok = chunk_decay >= -GLA_FAST_MAX_DECAY
    pl.when(fast_ok)(lambda: both(True))
    pl.when(jnp.logical_not(fast_ok))(lambda: both(False))


def _gla(gq, gk, gv, lgf, lgb, batch, seq):
    C = GLA_CHUNK
    nc = seq // C
    T = gq.shape[0]
    fwd = lambda w: pl.BlockSpec((C, w), lambda b, n: (b * nc + n, 0))
    bwd = lambda w: pl.BlockSpec((C, w), lambda b, n: (b * nc + nc - 1 - n, 0))
    return pl.pallas_call(
        _gla_kernel,
        grid=(batch, nc),
        in_specs=[fwd(GLA_K_W), fwd(GLA_K_W), fwd(GLA_V_W), fwd(GLA_K_W),
                  bwd(GLA_K_W), bwd(GLA_K_W), bwd(GLA_V_W), bwd(GLA_K_W)],
        out_specs=[fwd(GLA_V_W), bwd(GLA_V_W)],
        out_shape=[jax.ShapeDtypeStruct((T, GLA_V_W), F32)] * 2,
        scratch_shapes=[pltpu.VMEM((GLA_DV, GLA_K_W), F32), pltpu.VMEM((GLA_DV, GLA_K_W), F32),
                        pltpu.VMEM((C, GLA_V_W), F32), pltpu.VMEM((C, GLA_K_W), F32),
                        pltpu.VMEM((C, GLA_V_W), F32)],
        compiler_params=pltpu.CompilerParams(dimension_semantics=("arbitrary", "arbitrary"),
                                             vmem_limit_bytes=VMEM_LIMIT),
        name="gla_bidir",
    )(gq, gk, gv, lgf, gq, gk, gv, lgb)


def _layer_norm(y, g, b):
    yc = y - jnp.mean(y, axis=-1, keepdims=True)
    var = jnp.mean(yc * yc, axis=-1, keepdims=True)
    return yc * lax.rsqrt(var + LN_EPS) * g + b


def _post_mixer_kernel(of_ref, ob_ref, gr_ref, att_ref, x_ref, gng_ref, wo_ref, g1_ref, b1_ref,
                       wrh_ref, wrl_ref, br_ref, x1_ref, ei_ref, gw_ref):
    tm = x_ref.shape[0]
    m = _dot(att_ref[...], wo_ref[0:ATT_Q_W, :])
    for h in range(GLA_HEADS):
        vs = slice(h * GLA_DV, (h + 1) * GLA_DV)
        o = of_ref[:, vs] + ob_ref[:, vs]
        o = o * lax.rsqrt(jnp.mean(o * o, axis=-1, keepdims=True) + LN_EPS) * gng_ref[...]
        r = gr_ref[:, vs].astype(F32)
        gated = o * (r / (1.0 + jnp.exp(-r)))
        m = m + _dot(gated.astype(BF16), wo_ref[ATT_Q_W + h * GLA_DV:ATT_Q_W + (h + 1) * GLA_DV, :])
    x1 = _layer_norm(DEEPNORM_ALPHA * x_ref[...] + m, g1_ref[...], b1_ref[...])
    _store_token_tiles(x1_ref, x1)

    xh, xl = _split_bf16(x1)
    logits = _dot(xh, wrh_ref[...]) + _dot(xl, wrh_ref[...]) + _dot(xh, wrl_ref[...]) + br_ref[...]
    lane = lax.broadcasted_iota(jnp.int32, (tm, N_EXPERTS), 1).astype(F32)
    out_lane = lax.broadcasted_iota(jnp.int32, (tm, LANES), 1)
    vals = logits
    top_v = []
    e_out = jnp.zeros((tm, LANES), jnp.int32)
    for kk in range(TOP_K):
        mx = jnp.max(vals, axis=1, keepdims=True)
        idx = jnp.min(jnp.where(vals == mx, lane, float(N_EXPERTS)), axis=1, keepdims=True)
        top_v.append(mx)
        e_out = jnp.where(out_lane == kk, idx.astype(jnp.int32), e_out)
        vals = jnp.where(lane == idx, -jnp.inf, vals)
    ex = [jnp.exp(v - top_v[0]) for v in top_v]
    den = ex[0] + ex[1] + ex[2] + ex[3]
    g_out = jnp.zeros((tm, LANES), F32)
    for kk in range(TOP_K):
        g_out = jnp.where(out_lane == kk, ex[kk] / den, g_out)
    ei_ref[...] = e_out
    gw_ref[...] = g_out


def _post_mixer(o_f, o_b, gr, att, x2, gng, w_out, g1, b1, wr_hi, wr_lo, br):
    T = x2.shape[0]
    tm = PROJ_TILE
    row = lambda w: pl.BlockSpec((tm, w), lambda i: (i, 0))
    full = lambda a: pl.BlockSpec(a.shape, lambda i: (0,) * a.ndim)
    return pl.pallas_call(
        _post_mixer_kernel,
        grid=(T // tm,),
        in_specs=[row(GLA_V_W), row(GLA_V_W), row(GLA_V_W), row(ATT_Q_W), row(D_MODEL), full(gng), full(w_out),
                  full(g1), full(b1), full(wr_hi), full(wr_lo), full(br)],
        out_specs=[_token_tile_spec(tm), row(LANES), row(LANES)],
        out_shape=[jax.ShapeDtypeStruct((T * TOKEN_TILE_ROWS, LANES), F32), jax.ShapeDtypeStruct((T, LANES), jnp.int32),
                   jax.ShapeDtypeStruct((T, LANES), F32)],
        compiler_params=pltpu.CompilerParams(dimension_semantics=("arbitrary",), vmem_limit_bytes=VMEM_LIMIT),
        name="post_mixer_router",
    )(o_f, o_b, gr, att, x2, gng, w_out, g1, b1, wr_hi, wr_lo, br)


def _moe_kernel(be_ref, dprev_ref, scur_ref, snext_ref, x_hbm, wgu_ref, bgu_ref, wd_ref, bd_ref, y_hbm,
                xbuf, ybuf, gsem, ssem, *, plane):
    del be_ref
    R = MOE_BLOCK
    TR = TOKEN_TILE_ROWS
    i = pl.program_id(0)

    def tile(ref, row):
        return ref.at[pl.ds(pl.multiple_of(row * TR, TR), TR), :]

    def gather_row(idx_ref, r, s):
        tok = idx_ref[0, 0, r] & (plane - 1)
        return pltpu.make_async_copy(tile(x_hbm, tok), tile(xbuf.at[s], r), gsem.at[s])

    def scatter_row(idx_ref, r, s):
        return pltpu.make_async_copy(tile(ybuf.at[s], r), tile(y_hbm, idx_ref[0, 0, r]), ssem.at[s])

    def wait_gather(s):
        pltpu.make_async_copy(x_hbm.at[pl.ds(0, R * TR), :], xbuf.at[s], gsem.at[s]).wait()

    def wait_scatter(s):
        pltpu.make_async_copy(ybuf.at[s], y_hbm.at[pl.ds(0, R * TR), :], ssem.at[s]).wait()

    @pl.when(i == 0)
    def _():
        def issue(r, carry):
            gather_row(scur_ref, r, 0).start()
            return carry

        lax.fori_loop(0, R, issue, 0)
        ybuf[1] = jnp.zeros((R * TR, LANES), F32)

    def step(slot):
        other = 1 - slot
        wait_gather(slot)

        @pl.when(i > 0)
        def _():
            wait_scatter(slot)

        for r in range(R):
            gather_row(snext_ref, r, other).start()
            scatter_row(dprev_ref, r, other).start()

        xb = _load_token_tiles(xbuf.at[slot], R).astype(BF16)
        gu = _dot(xb, wgu_ref[0]) + bgu_ref[0]
        gate = jnp.minimum(gu[:, :D_FF], SWIGLU_LIMIT)
        up = jnp.clip(gu[:, D_FF:], -SWIGLU_LIMIT, SWIGLU_LIMIT)
        act = (up + 1.0) * gate / (1.0 + jnp.exp(-SWIGLU_ALPHA * gate))
        _store_token_tiles(ybuf.at[slot], _dot(act.astype(BF16), wd_ref[0]) + bd_ref[0])

        @pl.when(i == pl.num_programs(0) - 1)
        def _():
            wait_gather(other)
            wait_scatter(other)

    pl.when(i % 2 == 0)(lambda: step(0))
    pl.when(i % 2 == 1)(lambda: step(1))


def _moe(x1t, block_e, row_code, w_gu, b_gu, w_dn, b_dn, plane):
    R = MOE_BLOCK
    TR = TOKEN_TILE_ROWS
    steps = block_e.shape[0]
    smem_blk = lambda off: pl.BlockSpec((1, 1, R), lambda i, be: (i + off, 0, 0), memory_space=pltpu.SMEM)
    grid_spec = pltpu.PrefetchScalarGridSpec(
        num_scalar_prefetch=1,
        grid=(steps,),
        in_specs=[
            smem_blk(0), smem_blk(1), smem_blk(2),
            pl.BlockSpec(memory_space=pl.ANY),
            pl.BlockSpec((1, D_MODEL, 2 * D_FF), lambda i, be: (be[i], 0, 0)),
            pl.BlockSpec((1, 1, 2 * D_FF), lambda i, be: (be[i], 0, 0)),
            pl.BlockSpec((1, D_FF, D_MODEL), lambda i, be: (be[i], 0, 0)),
            pl.BlockSpec((1, 1, D_MODEL), lambda i, be: (be[i], 0, 0)),
        ],
        out_specs=pl.BlockSpec(memory_space=pl.ANY),
        scratch_shapes=[pltpu.VMEM((2, R * TR, LANES), F32), pltpu.VMEM((2, R * TR, LANES), F32),
                        pltpu.SemaphoreType.DMA((2,)), pltpu.SemaphoreType.DMA((2,))],
    )
    return pl.pallas_call(
        functools.partial(_moe_kernel, plane=plane),
        grid_spec=grid_spec,
        out_shape=jax.ShapeDtypeStruct(((TOP_K * plane + R) * TR, LANES), F32),
        compiler_params=pltpu.CompilerParams(dimension_semantics=("arbitrary",), vmem_limit_bytes=VMEM_LIMIT),
        name="moe_experts",
    )(block_e, row_code, row_code, row_code, x1t, w_gu, b_gu, w_dn, b_dn)


def _combine_kernel(y0_ref, y1_ref, y2_ref, y3_ref, gw_ref, x1_ref, g2_ref, b2_ref, out_ref):
    tm = out_ref.shape[0]
    gw = gw_ref[...]
    f = gw[:, 0:1] * _load_token_tiles(y0_ref, tm)
    for kk, y_ref in enumerate((y1_ref, y2_ref, y3_ref), start=1):
        f = f + gw[:, kk:kk + 1] * _load_token_tiles(y_ref, tm)
    out_ref[...] = _layer_norm(DEEPNORM_ALPHA * _load_token_tiles(x1_ref, tm) + f, g2_ref[...], b2_ref[...])


def _combine(y4, gw, x1t, g2, b2, plane):
    T = gw.shape[0]
    tm = COMBINE_TILE
    plane_tiles = plane // tm
    row = lambda w: pl.BlockSpec((tm, w), lambda i: (i, 0))
    full = lambda a: pl.BlockSpec(a.shape, lambda i: (0,) * a.ndim)
    slot_plane = lambda kk: _token_tile_spec(tm, lambda i: (kk * plane_tiles + i, 0))
    return pl.pallas_call(
        _combine_kernel,
        grid=(T // tm,),
        in_specs=[slot_plane(0), slot_plane(1), slot_plane(2), slot_plane(3), row(LANES), _token_tile_spec(tm),
                  full(g2), full(b2)],
        out_specs=row(D_MODEL),
        out_shape=jax.ShapeDtypeStruct((T, D_MODEL), F32),
        compiler_params=pltpu.CompilerParams(dimension_semantics=("arbitrary",), vmem_limit_bytes=VMEM_LIMIT),
        name="moe_combine_ln",
    )(y4, y4, y4, y4, gw, x1t, g2, b2)


def _routing_tables(e_idx, plane):
    T = e_idx.shape[0]
    R, E = MOE_BLOCK, N_EXPERTS
    M = T * TOP_K
    multi_hot = jnp.sum((e_idx[:, :, None] == jnp.arange(E, dtype=jnp.int32)).astype(jnp.int32), axis=1)
    before = jnp.cumsum(multi_hot, axis=0) - multi_hot
    rank = jnp.take_along_axis(before, e_idx, axis=1)
    counts = before[-1] + multi_hot[-1]
    padded = ((counts + R - 1) // R) * R
    ends_pad = jnp.cumsum(padded)
    starts_pad = ends_pad - padded
    dest = starts_pad[e_idx] + rank
    n_blocks = M // R + E
    code = jnp.arange(T, dtype=jnp.int32)[:, None] + (jnp.arange(TOP_K, dtype=jnp.int32) * plane)[None, :]
    spare = TOP_K * plane + jnp.arange(R, dtype=jnp.int32)
    row_code = jnp.tile(spare, n_blocks + 3).at[dest.reshape(-1) + R].set(code.reshape(-1), unique_indices=True)
    block_start = jnp.arange(n_blocks + 1, dtype=jnp.int32) * R
    block_e = jnp.minimum(jnp.sum(block_start[:, None] >= ends_pad[None, :], axis=1), E - 1).astype(jnp.int32)
    return row_code.reshape(n_blocks + 3, 1, R), block_e


def _rope_tables(seq):
    half = ATT_HEAD_DIM // 2
    inv_freq = ROPE_THETA ** (-jnp.arange(half, dtype=F32) * 2.0 / ATT_HEAD_DIM)
    ang = jnp.arange(seq, dtype=F32)[:, None] * inv_freq[None, :]
    cos, sin = jnp.cos(ang), jnp.sin(ang)
    reps = LANES // ATT_HEAD_DIM
    cos_t = jnp.tile(jnp.concatenate([cos, cos], axis=1), (1, reps))
    sin_t = jnp.tile(jnp.concatenate([-sin, sin], axis=1), (1, reps))
    return cos_t, sin_t


def _run_trunk(x, p):
    B, S, D = x.shape
    T = B * S
    plane = max(1 << (T - 1).bit_length(), COMBINE_TILE)
    assert T % COMBINE_TILE == 0 and T >= MOE_BLOCK
    x2 = x.reshape(T, D)
    cos_t, sin_t = _rope_tables(S)
    aq, ak, av, gq, gk, gv, gr, lgf, lgb = _in_proj(x2, p["w_in"], p["wgf"], p["bgf"], p["wgb"], p["bgb"],
                                                    cos_t, sin_t, S)
    att = _attention(aq, ak, av, p["sinks"], B, S)
    o_f, o_b = _gla(gq, gk, gv, lgf, lgb, B, S)
    x1t, e_pad, g_pad = _post_mixer(o_f, o_b, gr, att, x2, p["gng"], p["w_out"], p["g1"], p["b1"],
                                    p["wr_hi"], p["wr_lo"], p["br"])
    row_code, block_e = _routing_tables(e_pad[:, :TOP_K], plane)
    y4 = _moe(x1t, block_e, row_code, p["w_gu"], p["b_gu"], p["w_dn"], p["b_dn"], plane)
    y = _combine(y4, g_pad, x1t, p["g2"], p["b2"], plane)
    return y.reshape(B, S, D)


def kernel(x_prompt, x_sample, w_in, w_gate_f, b_gate_f, w_gate_b, b_gate_b, sinks, gla_norm_g, w_out, ln1_g, ln1_b,
           w_router, b_router, w_gate_up, b_gate_up, w_down, b_down, ln2_g, ln2_b):
    l = 0
    wr = w_router[l]
    wr_hi = wr.astype(BF16)
    p = dict(
        w_in=w_in[l].astype(BF16),
        wgf=w_gate_f[l].astype(BF16), bgf=b_gate_f[l].reshape(1, -1),
        wgb=w_gate_b[l].astype(BF16), bgb=b_gate_b[l].reshape(1, -1),
        sinks=sinks[l], gng=gla_norm_g[l].reshape(1, -1), w_out=w_out[l].astype(BF16),
        g1=ln1_g[l].reshape(1, -1), b1=ln1_b[l].reshape(1, -1),
        wr_hi=wr_hi, wr_lo=(wr - wr_hi.astype(F32)).astype(BF16), br=b_router[l].reshape(1, -1),
        w_gu=w_gate_up[l].astype(BF16), b_gu=b_gate_up[l].reshape(N_EXPERTS, 1, -1),
        w_dn=w_down[l].astype(BF16), b_dn=b_down[l].reshape(N_EXPERTS, 1, -1),
        g2=ln2_g[l].reshape(1, -1), b2=ln2_b[l].reshape(1, -1),
    )
    return (_run_trunk(x_prompt, p), _run_trunk(x_sample, p))
```

```python
import functools

import jax
import jax.numpy as jnp
import numpy as np
from jax import lax
from jax.experimental import pallas as pl
from jax.experimental.pallas import tpu as pltpu

D_MODEL = 1024
DEPTH = 1
ATT_HEADS = 8
ATT_KV_HEADS = 2
ATT_HEAD_DIM = 64
ATT_GROUP = ATT_HEADS // ATT_KV_HEADS
WINDOW = 128
ATT_BLOCK = 128
ROPE_THETA = 10000.0
GLA_HEADS = 4
GLA_DK = 64
GLA_DV = 128
GLA_GATE_RANK = 16
GLA_TAU = 16.0
N_EXPERTS = 32
TOP_K = 4
D_FF = 1024
SWIGLU_LIMIT = 7.0
SWIGLU_ALPHA = 1.702
MOE_BLOCK = 128
LN_EPS = 1e-5
DEEPNORM_ALPHA = (2 * DEPTH) ** 0.25

ATT_Q_W = ATT_HEADS * ATT_HEAD_DIM
ATT_KV_W = ATT_KV_HEADS * ATT_HEAD_DIM
GLA_K_W = GLA_HEADS * GLA_DK
GLA_V_W = GLA_HEADS * GLA_DV
OFF_AQ = 0
OFF_AK = OFF_AQ + ATT_Q_W
OFF_AV = OFF_AK + ATT_KV_W
OFF_GQ = OFF_AV + ATT_KV_W
OFF_GK = OFF_GQ + GLA_K_W
OFF_GV = OFF_GK + GLA_K_W
OFF_GR = OFF_GV + GLA_V_W
OFF_Z = OFF_GR + GLA_V_W
IN_WIDTH = OFF_Z + 2 * GLA_GATE_RANK

LANES = 128
PROJ_TILE = 256
GLA_CHUNK = 128
GLA_FAST_MAX_DECAY = 60.0
COMBINE_TILE = 256
NEG_BIG = -1e30
VMEM_LIMIT = 48 * 1024 * 1024

BF16 = jnp.bfloat16
F32 = jnp.float32


def _dot(a, b):
    return jnp.dot(a, b, preferred_element_type=F32)


def _dot_nt(a, b):
    return lax.dot_general(a, b, (((1,), (1,)), ((), ())), preferred_element_type=F32)


def _dot_tn(a, b):
    return lax.dot_general(a, b, (((0,), (0,)), ((), ())), preferred_element_type=F32)


def _split_bf16(x):
    hi = x.astype(BF16)
    lo = (x - hi.astype(F32)).astype(BF16)
    return hi, lo


TOKEN_TILE_ROWS = D_MODEL // LANES


def _token_tile_spec(tokens, index_map=lambda i: (i, 0)):
    return pl.BlockSpec((tokens * TOKEN_TILE_ROWS, LANES), index_map)


def _load_token_tiles(ref, tokens):
    return jnp.concatenate([ref[pl.ds(c, tokens, stride=TOKEN_TILE_ROWS), :] for c in range(TOKEN_TILE_ROWS)],
                           axis=1)


def _store_token_tiles(ref, x):
    for c in range(TOKEN_TILE_ROWS):
        ref[pl.ds(c, x.shape[0], stride=TOKEN_TILE_ROWS), :] = x[:, c * LANES:(c + 1) * LANES]


def _in_proj_kernel(x_ref, w_ref, wgf_ref, bgf_ref, wgb_ref, bgb_ref, cos_ref, sin_ref,
                    aq_ref, ak_ref, av_ref, gq_ref, gk_ref, gv_ref, gr_ref, lgf_ref, lgb_ref):
    xb = x_ref[...].astype(BF16)
    cos = cos_ref[...]
    sin = sin_ref[...]
    lane = lax.broadcasted_iota(jnp.int32, (x_ref.shape[0], LANES), 1)
    first_half = (lane % ATT_HEAD_DIM) < (ATT_HEAD_DIM // 2)

    def rope(t):
        swapped = jnp.where(first_half, pltpu.roll(t, LANES - ATT_HEAD_DIM // 2, 1),
                            pltpu.roll(t, ATT_HEAD_DIM // 2, 1))
        return t * cos + swapped * sin

    q_scale = ATT_HEAD_DIM ** -0.5
    for c in range(ATT_Q_W // LANES):
        t = _dot(xb, w_ref[:, OFF_AQ + c * LANES:OFF_AQ + (c + 1) * LANES])
        aq_ref[:, c * LANES:(c + 1) * LANES] = (rope(t) * q_scale).astype(aq_ref.dtype)
    ak_ref[...] = rope(_dot(xb, w_ref[:, OFF_AK:OFF_AK + ATT_KV_W])).astype(ak_ref.dtype)
    av_ref[...] = _dot(xb, w_ref[:, OFF_AV:OFF_AV + ATT_KV_W]).astype(av_ref.dtype)
    gq_ref[...] = _dot(xb, w_ref[:, OFF_GQ:OFF_GQ + GLA_K_W]) * (GLA_DK ** -0.5)
    gk_ref[...] = _dot(xb, w_ref[:, OFF_GK:OFF_GK + GLA_K_W])
    gv_ref[...] = _dot(xb, w_ref[:, OFF_GV:OFF_GV + GLA_V_W]).astype(gv_ref.dtype)
    gr_ref[...] = _dot(xb, w_ref[:, OFF_GR:OFF_GR + GLA_V_W]).astype(gr_ref.dtype)
    z = _dot(xb, w_ref[:, OFF_Z:OFF_Z + 2 * GLA_GATE_RANK])

    def log_decay(zr, wg_ref, bg_ref):
        pre = _dot(zr.astype(BF16), wg_ref[...]) + bg_ref[...]
        log_sig = jnp.minimum(pre, 0.0) - jnp.log(1.0 + jnp.exp(-jnp.abs(pre)))
        return log_sig * (1.0 / GLA_TAU)

    lgf_ref[...] = log_decay(z[:, :GLA_GATE_RANK], wgf_ref, bgf_ref)
    lgb_ref[...] = log_decay(z[:, GLA_GATE_RANK:], wgb_ref, bgb_ref)


def _in_proj(x2, w_in, wgf, bgf, wgb, bgb, cos_t, sin_t, seq):
    T = x2.shape[0]
    tm = PROJ_TILE
    assert T % tm == 0 and seq % tm == 0
    pos_tiles = seq // tm
    row = lambda w: pl.BlockSpec((tm, w), lambda i: (i, 0))
    full = lambda a: pl.BlockSpec(a.shape, lambda i: (0,) * a.ndim)
    pos = pl.BlockSpec((tm, LANES), lambda i: (i % pos_tiles, 0))
    out_w = [(ATT_Q_W, BF16), (ATT_KV_W, BF16), (ATT_KV_W, BF16), (GLA_K_W, F32), (GLA_K_W, F32),
             (GLA_V_W, BF16), (GLA_V_W, BF16), (GLA_K_W, F32), (GLA_K_W, F32)]
    return pl.pallas_call(
        _in_proj_kernel,
        grid=(T // tm,),
        in_specs=[row(D_MODEL), full(w_in), full(wgf), full(bgf), full(wgb), full(bgb), pos, pos],
        out_specs=[row(w) for w, _ in out_w],
        out_shape=[jax.ShapeDtypeStruct((T, w), dt) for w, dt in out_w],
        compiler_params=pltpu.CompilerParams(dimension_semantics=("arbitrary",), vmem_limit_bytes=VMEM_LIMIT),
        name="in_proj",
    )(x2, w_in, wgf, bgf, wgb, bgb, cos_t, sin_t)


def _attention_kernel(sinks_ref, q_ref, kp_ref, kc_ref, kn_ref, vp_ref, vc_ref, vn_ref, o_ref, *, seq):
    n = pl.program_id(1)
    R = ATT_BLOCK
    D = ATT_HEAD_DIM
    kcat = jnp.concatenate([kp_ref[...], kc_ref[...], kn_ref[...]], axis=0)
    vcat = jnp.concatenate([vp_ref[...], vc_ref[...], vn_ref[...]], axis=0)
    qi = lax.broadcasted_iota(jnp.int32, (R, 3 * R), 0)
    kj = lax.broadcasted_iota(jnp.int32, (R, 3 * R), 1)
    rel = kj - R - qi
    key_pos = n * R - R + kj
    valid = (jnp.abs(rel) <= WINDOW) & (key_pos >= 0) & (key_pos < seq)
    assert ATT_KV_HEADS * D == LANES
    v_lane = lax.broadcasted_iota(jnp.int32, vcat.shape, 1)
    v_aug = [jnp.where((v_lane < D) == (kv == 0), vcat, jnp.ones_like(vcat)) for kv in range(ATT_KV_HEADS)]

    scores = []
    for h in range(ATT_HEADS):
        kv = h // ATT_GROUP
        s = _dot_nt(q_ref[:, h * D:(h + 1) * D], kcat[:, kv * D:(kv + 1) * D])
        scores.append(jnp.where(valid, s, NEG_BIG))
    probs, sink_terms = [], []
    for h in range(ATT_HEADS):
        sink = sinks_ref[h]
        m = jnp.maximum(jnp.max(scores[h], axis=1, keepdims=True), sink)
        probs.append(jnp.exp(scores[h] - m).astype(BF16))
        sink_terms.append(jnp.exp(sink - m))
    for h in range(ATT_HEADS):
        kv = h // ATT_GROUP
        res = _dot(probs[h], v_aug[kv])
        val = res[:, kv * D:(kv + 1) * D]
        den = res[:, (1 - kv) * D:(1 - kv) * D + 1] + sink_terms[h]
        o_ref[:, h * D:(h + 1) * D] = (val / den).astype(o_ref.dtype)


def _attention(aq, ak, av, sinks, batch, seq):
    R = ATT_BLOCK
    nb = seq // R
    qspec = pl.BlockSpec((R, ATT_Q_W), lambda b, n: (b * nb + n, 0))
    prev = pl.BlockSpec((R, ATT_KV_W), lambda b, n: (b * nb + jnp.maximum(n - 1, 0), 0))
    cur = pl.BlockSpec((R, ATT_KV_W), lambda b, n: (b * nb + n, 0))
    nxt = pl.BlockSpec((R, ATT_KV_W), lambda b, n: (b * nb + jnp.minimum(n + 1, nb - 1), 0))
    return pl.pallas_call(
        functools.partial(_attention_kernel, seq=seq),
        grid=(batch, nb),
        in_specs=[pl.BlockSpec(memory_space=pltpu.SMEM), qspec, prev, cur, nxt, prev, cur, nxt],
        out_specs=qspec,
        out_shape=jax.ShapeDtypeStruct(aq.shape, BF16),
        compiler_params=pltpu.CompilerParams(dimension_semantics=("arbitrary", "arbitrary"),
                                             vmem_limit_bytes=VMEM_LIMIT),
        name="swa_attention",
    )(sinks, aq, ak, ak, ak, av, av, av)


_GLA_HEAD_SLICES = [(slice(h * GLA_DK, (h + 1) * GLA_DK), slice(h * GLA_DV, (h + 1) * GLA_DV)) for h in range(GLA_HEADS)]


def _gla_decays(q_ref, k_ref, lg_ref, reverse):
    C = q_ref.shape[0]
    ri = lax.broadcasted_iota(jnp.int32, (C, C), 0)
    cj = lax.broadcasted_iota(jnp.int32, (C, C), 1)
    pair = (cj >= ri) if reverse else (cj <= ri)
    tri = pair.astype(BF16)
    lg_hi, lg_lo = _split_bf16(lg_ref[...])
    bc = _dot(tri, lg_hi) + _dot(tri, lg_lo)
    end = 0 if reverse else C - 1
    b_end = bc[end:end + 1, :]
    qt = (q_ref[...] * jnp.exp(bc)).astype(BF16)
    k_to_end = (k_ref[...] * jnp.exp(b_end - bc)).astype(BF16)
    return pair, bc, b_end, qt, k_to_end


def _gla_exact_intra(q_ref, k_ref, v_ref, bc, oi_ref, bc_ref, vf_ref, reverse):
    C = q_ref.shape[0]
    q = q_ref[...]
    bc_ref[...] = bc
    vf_ref[...] = v_ref[...].astype(F32)
    oi_ref[...] = jnp.zeros_like(oi_ref)
    rows = lax.broadcasted_iota(jnp.int32, (C, 1), 0)

    def body(j, carry):
        kj = k_ref[pl.ds(j, 1), :]
        bj = bc_ref[pl.ds(j, 1), :]
        vj = vf_ref[pl.ds(j, 1), :]
        live = (rows <= j) if reverse else (rows >= j)
        t = jnp.where(live, q * kj * jnp.exp(jnp.minimum(bc - bj, 0.0)), 0.0)
        for ks, vs in _GLA_HEAD_SLICES:
            a = jnp.sum(t[:, ks], axis=1, keepdims=True)
            oi_ref[:, vs] += a * vj[:, vs]
        return carry

    lax.fori_loop(0, C, body, 0)
    return [oi_ref[:, vs] for _, vs in _GLA_HEAD_SLICES]


def _gla_kernel(qf_ref, kf_ref, vf_ref, lgf_ref, qb_ref, kb_ref, vb_ref, lgb_ref, of_ref, ob_ref,
                stf_ref, stb_ref, oi_ref, bc_ref, vv_ref):
    @pl.when(pl.program_id(1) == 0)
    def _():
        stf_ref[...] = jnp.zeros_like(stf_ref)
        stb_ref[...] = jnp.zeros_like(stb_ref)

    dirs = [(qf_ref, kf_ref, vf_ref, lgf_ref, of_ref, stf_ref, False),
            (qb_ref, kb_ref, vb_ref, lgb_ref, ob_ref, stb_ref, True)]

    def both(fast):
        dec = [_gla_decays(q_ref, k_ref, lg_ref, rev) for q_ref, k_ref, _, lg_ref, _, _, rev in dirs]
        if fast:
            scores = []
            for (_, k_ref, _, _, _, _, _), (_, bc, _, qt, _) in zip(dirs, dec):
                kt = (k_ref[...] * jnp.exp(-bc)).astype(BF16)
                scores.append([_dot_nt(qt[:, ks], kt[:, ks]) for ks, _ in _GLA_HEAD_SLICES])
        inter, new_state = [], []
        for (_, _, v_ref, _, _, st_ref, _), (_, _, b_end, qt, k_to_end) in zip(dirs, dec):
            st = st_ref[...]
            st_b = st.astype(BF16)
            inter.append([_dot_nt(qt[:, ks], st_b[:, ks]) for ks, _ in _GLA_HEAD_SLICES])
            cols = [_dot_tn(v_ref[:, vs], k_to_end[:, ks]) for ks, vs in _GLA_HEAD_SLICES]
            new_state.append(st * jnp.exp(b_end) + jnp.concatenate(cols, axis=1))
        for d, ((q_ref, k_ref, v_ref, _, o_ref, st_ref, rev), (pair, bc, _, _, _)) in enumerate(zip(dirs, dec)):
            if fast:
                intra = [_dot(jnp.where(pair, a, 0.0).astype(BF16), v_ref[:, vs])
                         for a, (_, vs) in zip(scores[d], _GLA_HEAD_SLICES)]
            else:
                intra = _gla_exact_intra(q_ref, k_ref, v_ref, bc, oi_ref, bc_ref, vv_ref, rev)
            for (_, vs), o_inter, o_intra in zip(_GLA_HEAD_SLICES, inter[d], intra):
                o_ref[:, vs] = o_inter + o_intra
            st_ref[...] = new_state[d]

    chunk_decay = jnp.minimum(jnp.min(jnp.sum(lgf_ref[...], axis=0, keepdims=True)),
                              jnp.min(jnp.sum(lgb_ref[...], axis=0, keepdims=True)))
    fast_ok = chunk_decay >= -GLA_FAST_MAX_DECAY
    pl.when(fast_ok)(lambda: both(True))
    pl.when(jnp.logical_not(fast_ok))(lambda: both(False))


def _gla(gq, gk, gv, lgf, lgb, batch, seq):
    C = GLA_CHUNK
    nc = seq // C
    T = gq.shape[0]
    fwd = lambda w: pl.BlockSpec((C, w), lambda b, n: (b * nc + n, 0))
    bwd = lambda w: pl.BlockSpec((C, w), lambda b, n: (b * nc + nc - 1 - n, 0))
    return pl.pallas_call(
        _gla_kernel,
        grid=(batch, nc),
        in_specs=[fwd(GLA_K_W), fwd(GLA_K_W), fwd(GLA_V_W), fwd(GLA_K_W),
                  bwd(GLA_K_W), bwd(GLA_K_W), bwd(GLA_V_W), bwd(GLA_K_W)],
        out_specs=[fwd(GLA_V_W), bwd(GLA_V_W)],
        out_shape=[jax.ShapeDtypeStruct((T, GLA_V_W), F32)] * 2,
        scratch_shapes=[pltpu.VMEM((GLA_DV, GLA_K_W), F32), pltpu.VMEM((GLA_DV, GLA_K_W), F32),
                        pltpu.VMEM((C, GLA_V_W), F32), pltpu.VMEM((C, GLA_K_W), F32),
                        pltpu.VMEM((C, GLA_V_W), F32)],
        compiler_params=pltpu.CompilerParams(dimension_semantics=("arbitrary", "arbitrary"),
                                             vmem_limit_bytes=VMEM_LIMIT),
        name="gla_bidir",
    )(gq, gk, gv, lgf, gq, gk, gv, lgb)


def _layer_norm(y, g, b):
    yc = y - jnp.mean(y, axis=-1, keepdims=True)
    var = jnp.mean(yc * yc, axis=-1, keepdims=True)
    return yc * lax.rsqrt(var + LN_EPS) * g + b


def _post_mixer_kernel(of_ref, ob_ref, gr_ref, att_ref, x_ref, gng_ref, wo_ref, g1_ref, b1_ref,
                       wrh_ref, wrl_ref, br_ref, x1_ref, ei_ref, gw_ref):
    tm = x_ref.shape[0]
    m = _dot(att_ref[...], wo_ref[0:ATT_Q_W, :])
    for h in range(GLA_HEADS):
        vs = slice(h * GLA_DV, (h + 1) * GLA_DV)
        o = of_ref[:, vs] + ob_ref[:, vs]
        o = o * lax.rsqrt(jnp.mean(o * o, axis=-1, keepdims=True) + LN_EPS) * gng_ref[...]
        r = gr_ref[:, vs].astype(F32)
        gated = o * (r / (1.0 + jnp.exp(-r)))
        m = m + _dot(gated.astype(BF16), wo_ref[ATT_Q_W + h * GLA_DV:ATT_Q_W + (h + 1) * GLA_DV, :])
    x1 = _layer_norm(DEEPNORM_ALPHA * x_ref[...] + m, g1_ref[...], b1_ref[...])
    _store_token_tiles(x1_ref, x1)

    xh, xl = _split_bf16(x1)
    logits = _dot(xh, wrh_ref[...]) + _dot(xl, wrh_ref[...]) + _dot(xh, wrl_ref[...]) + br_ref[...]
    lane = lax.broadcasted_iota(jnp.int32, (tm, N_EXPERTS), 1).astype(F32)
    out_lane = lax.broadcasted_iota(jnp.int32, (tm, LANES), 1)
    vals = logits
    top_v = []
    e_out = jnp.zeros((tm, LANES), jnp.int32)
    for kk in range(TOP_K):
        mx = jnp.max(vals, axis=1, keepdims=True)
        idx = jnp.min(jnp.where(vals == mx, lane, float(N_EXPERTS)), axis=1, keepdims=True)
        top_v.append(mx)
        e_out = jnp.where(out_lane == kk, idx.astype(jnp.int32), e_out)
        vals = jnp.where(lane == idx, -jnp.inf, vals)
    ex = [jnp.exp(v - top_v[0]) for v in top_v]
    den = ex[0] + ex[1] + ex[2] + ex[3]
    g_out = jnp.zeros((tm, LANES), F32)
    for kk in range(TOP_K):
        g_out = jnp.where(out_lane == kk, ex[kk] / den, g_out)
    ei_ref[...] = e_out
    gw_ref[...] = g_out


def _post_mixer(o_f, o_b, gr, att, x2, gng, w_out, g1, b1, wr_hi, wr_lo, br):
    T = x2.shape[0]
    tm = PROJ_TILE
    row = lambda w: pl.BlockSpec((tm, w), lambda i: (i, 0))
    full = lambda a: pl.BlockSpec(a.shape, lambda i: (0,) * a.ndim)
    return pl.pallas_call(
        _post_mixer_kernel,
        grid=(T // tm,),
        in_specs=[row(GLA_V_W), row(GLA_V_W), row(GLA_V_W), row(ATT_Q_W), row(D_MODEL), full(gng), full(w_out),
                  full(g1), full(b1), full(wr_hi), full(wr_lo), full(br)],
        out_specs=[_token_tile_spec(tm), row(LANES), row(LANES)],
        out_shape=[jax.ShapeDtypeStruct((T * TOKEN_TILE_ROWS, LANES), F32), jax.ShapeDtypeStruct((T, LANES), jnp.int32),
                   jax.ShapeDtypeStruct((T, LANES), F32)],
        compiler_params=pltpu.CompilerParams(dimension_semantics=("arbitrary",), vmem_limit_bytes=VMEM_LIMIT),
        name="post_mixer_router",
    )(o_f, o_b, gr, att, x2, gng, w_out, g1, b1, wr_hi, wr_lo, br)


MOE_SLOTS = 3
MOE_FF_CHUNK = 256


def _moe_kernel(be_ref, dprev_ref, scur_ref, snext_ref, snext2_ref, x_hbm, wgu_ref, bgu_ref, wd_ref, bd_ref, y_hbm,
                *scratch, plane):
    del be_ref
    R = MOE_BLOCK
    TR = TOKEN_TILE_ROWS
    i = pl.program_id(0)
    xbuf, ybuf = scratch[:MOE_SLOTS], scratch[MOE_SLOTS:2 * MOE_SLOTS]
    gsem, ssem = scratch[2 * MOE_SLOTS:]

    def tile(ref, row):
        return ref.at[pl.ds(pl.multiple_of(row * TR, TR), TR), :]

    def gather_row(idx_ref, r, s):
        tok = idx_ref[0, 0, r] & (plane - 1)
        return pltpu.make_async_copy(tile(x_hbm, tok), tile(xbuf[s], r), gsem.at[s])

    def scatter_row(idx_ref, r, s):
        return pltpu.make_async_copy(tile(ybuf[s], r), tile(y_hbm, idx_ref[0, 0, r]), ssem.at[s])

    def wait_gather(s):
        pltpu.make_async_copy(x_hbm.at[pl.ds(0, R * TR), :], xbuf[s], gsem.at[s]).wait()

    def wait_scatter(s):
        pltpu.make_async_copy(ybuf[s], y_hbm.at[pl.ds(0, R * TR), :], ssem.at[s]).wait()

    @pl.when(i == 0)
    def _():
        def issue(r, carry):
            gather_row(scur_ref, r, 0).start()
            gather_row(snext_ref, r, 1).start()
            return carry

        lax.fori_loop(0, R, issue, 0)
        ybuf[MOE_SLOTS - 1][...] = jnp.zeros((R * TR, LANES), F32)

    def step(slot):
        ahead = (slot + 2) % MOE_SLOTS
        wait_gather(slot)

        @pl.when(i >= MOE_SLOTS - 1)
        def _():
            wait_scatter(slot)

        for r in range(R):
            gather_row(snext2_ref, r, ahead).start()
            scatter_row(dprev_ref, r, ahead).start()

        xb = _load_token_tiles(xbuf[slot], R).astype(BF16)
        acts = []
        for c in range(D_FF // MOE_FF_CHUNK):
            gs = slice(c * MOE_FF_CHUNK, (c + 1) * MOE_FF_CHUNK)
            us = slice(D_FF + c * MOE_FF_CHUNK, D_FF + (c + 1) * MOE_FF_CHUNK)
            gate = jnp.minimum(_dot(xb, wgu_ref[0, :, gs]) + bgu_ref[0, :, gs], SWIGLU_LIMIT)
            up = jnp.clip(_dot(xb, wgu_ref[0, :, us]) + bgu_ref[0, :, us], -SWIGLU_LIMIT, SWIGLU_LIMIT)
            acts.append(((up + 1.0) * gate / (1.0 + jnp.exp(-SWIGLU_ALPHA * gate))).astype(BF16))
        act = jnp.concatenate(acts, axis=1)
        _store_token_tiles(ybuf[slot], _dot(act, wd_ref[0]) + bd_ref[0])

        @pl.when(i == pl.num_programs(0) - 1)
        def _():
            wait_gather((slot + 1) % MOE_SLOTS)
            wait_gather(ahead)
            wait_scatter((slot + 1) % MOE_SLOTS)
            wait_scatter(ahead)

    for slot in range(MOE_SLOTS):
        pl.when(i % MOE_SLOTS == slot)(functools.partial(step, slot))


def _moe(x1t, block_e, row_code, w_gu, b_gu, w_dn, b_dn, plane):
    R = MOE_BLOCK
    TR = TOKEN_TILE_ROWS
    steps = block_e.shape[0]
    smem_blk = lambda off: pl.BlockSpec((1, 1, R), lambda i, be: (i + off, 0, 0), memory_space=pltpu.SMEM)
    grid_spec = pltpu.PrefetchScalarGridSpec(
        num_scalar_prefetch=1,
        grid=(steps,),
        in_specs=[
            smem_blk(0), smem_blk(1), smem_blk(2), smem_blk(3),
            pl.BlockSpec(memory_space=pl.ANY),
            pl.BlockSpec((1, D_MODEL, 2 * D_FF), lambda i, be: (be[i], 0, 0)),
            pl.BlockSpec((1, 1, 2 * D_FF), lambda i, be: (be[i], 0, 0)),
            pl.BlockSpec((1, D_FF, D_MODEL), lambda i, be: (be[i], 0, 0)),
            pl.BlockSpec((1, 1, D_MODEL), lambda i, be: (be[i], 0, 0)),
        ],
        out_specs=pl.BlockSpec(memory_space=pl.ANY),
        scratch_shapes=[pltpu.VMEM((R * TR, LANES), F32)] * (2 * MOE_SLOTS)
        + [pltpu.SemaphoreType.DMA((MOE_SLOTS,)), pltpu.SemaphoreType.DMA((MOE_SLOTS,))],
    )
    return pl.pallas_call(
        functools.partial(_moe_kernel, plane=plane),
        grid_spec=grid_spec,
        out_shape=jax.ShapeDtypeStruct(((TOP_K * plane + 2 * R) * TR, LANES), F32),
        compiler_params=pltpu.CompilerParams(dimension_semantics=("arbitrary",), vmem_limit_bytes=VMEM_LIMIT),
        name="moe_experts",
    )(block_e, row_code, row_code, row_code, row_code, x1t, w_gu, b_gu, w_dn, b_dn)


def _combine_kernel(y0_ref, y1_ref, y2_ref, y3_ref, gw_ref, x1_ref, g2_ref, b2_ref, out_ref):
    tm = out_ref.shape[0]
    gw = gw_ref[...]
    f = gw[:, 0:1] * _load_token_tiles(y0_ref, tm)
    for kk, y_ref in enumerate((y1_ref, y2_ref, y3_ref), start=1):
        f = f + gw[:, kk:kk + 1] * _load_token_tiles(y_ref, tm)
    out_ref[...] = _layer_norm(DEEPNORM_ALPHA * _load_token_tiles(x1_ref, tm) + f, g2_ref[...], b2_ref[...])


def _combine(y4, gw, x1t, g2, b2, plane):
    T = gw.shape[0]
    tm = COMBINE_TILE
    plane_tiles = plane // tm
    row = lambda w: pl.BlockSpec((tm, w), lambda i: (i, 0))
    full = lambda a: pl.BlockSpec(a.shape, lambda i: (0,) * a.ndim)
    slot_plane = lambda kk: _token_tile_spec(tm, lambda i: (kk * plane_tiles + i, 0))
    return pl.pallas_call(
        _combine_kernel,
        grid=(T // tm,),
        in_specs=[slot_plane(0), slot_plane(1), slot_plane(2), slot_plane(3), row(LANES), _token_tile_spec(tm),
                  full(g2), full(b2)],
        out_specs=row(D_MODEL),
        out_shape=jax.ShapeDtypeStruct((T, D_MODEL), F32),
        compiler_params=pltpu.CompilerParams(dimension_semantics=("arbitrary",), vmem_limit_bytes=VMEM_LIMIT),
        name="moe_combine_ln",
    )(y4, y4, y4, y4, gw, x1t, g2, b2)


def _routing_tables(e_idx, plane):
    T = e_idx.shape[0]
    R, E = MOE_BLOCK, N_EXPERTS
    M = T * TOP_K
    m_bits = (M - 1).bit_length()
    assert (E << m_bits) < 2 ** 31
    flat_e = e_idx.reshape(M)
    m_sorted = jnp.sort((flat_e << m_bits) + jnp.arange(M, dtype=jnp.int32)) & ((1 << m_bits) - 1)
    code_sorted = (m_sorted % TOP_K) * plane + m_sorted // TOP_K
    counts = jnp.sum((flat_e[:, None] == jnp.arange(E, dtype=jnp.int32)[None, :]).astype(jnp.int32), axis=0)
    padded = ((counts + R - 1) // R) * R
    ends_pad = jnp.cumsum(padded)
    starts_pad = ends_pad - padded
    starts = jnp.cumsum(counts) - counts
    n_blocks = M // R + E
    block_start = jnp.arange(n_blocks + 1, dtype=jnp.int32) * R
    block_e = jnp.minimum(jnp.sum(block_start[:, None] >= ends_pad[None, :], axis=1), E - 1).astype(jnp.int32)
    be = block_e[:n_blocks]
    in_row = jnp.arange(R, dtype=jnp.int32)[None, :]
    local = (block_start[:n_blocks] - starts_pad[be])[:, None] + in_row
    valid = local < counts[be][:, None]
    src = jnp.minimum(starts[be][:, None] + local, M - 1)
    parity = (jnp.arange(n_blocks + 4, dtype=jnp.int32) % 2)[:, None]
    spare = TOP_K * plane + parity * R + in_row
    body = jnp.where(valid, code_sorted[src], spare[1:n_blocks + 1])
    row_code = jnp.concatenate([spare[:1], body, spare[n_blocks + 1:]], axis=0)
    return row_code.reshape(n_blocks + 4, 1, R), block_e


def _rope_tables(seq):
    half = ATT_HEAD_DIM // 2
    inv_freq = ROPE_THETA ** (-jnp.arange(half, dtype=F32) * 2.0 / ATT_HEAD_DIM)
    ang = jnp.arange(seq, dtype=F32)[:, None] * inv_freq[None, :]
    cos, sin = jnp.cos(ang), jnp.sin(ang)
    reps = LANES // ATT_HEAD_DIM
    cos_t = jnp.tile(jnp.concatenate([cos, cos], axis=1), (1, reps))
    sin_t = jnp.tile(jnp.concatenate([-sin, sin], axis=1), (1, reps))
    return cos_t, sin_t


def _run_trunk(x, p):
    B, S, D = x.shape
    T = B * S
    plane = max(1 << (T - 1).bit_length(), COMBINE_TILE)
    assert T % COMBINE_TILE == 0 and T >= 2 * MOE_BLOCK
    x2 = x.reshape(T, D)
    cos_t, sin_t = _rope_tables(S)
    aq, ak, av, gq, gk, gv, gr, lgf, lgb = _in_proj(x2, p["w_in"], p["wgf"], p["bgf"], p["wgb"], p["bgb"],
                                                    cos_t, sin_t, S)
    att = _attention(aq, ak, av, p["sinks"], B, S)
    o_f, o_b = _gla(gq, gk, gv, lgf, lgb, B, S)
    x1t, e_pad, g_pad = _post_mixer(o_f, o_b, gr, att, x2, p["gng"], p["w_out"], p["g1"], p["b1"],
                                    p["wr_hi"], p["wr_lo"], p["br"])
    row_code, block_e = _routing_tables(e_pad[:, :TOP_K], plane)
    y4 = _moe(x1t, block_e, row_code, p["w_gu"], p["b_gu"], p["w_dn"], p["b_dn"], plane)
    y = _combine(y4, g_pad, x1t, p["g2"], p["b2"], plane)
    return y.reshape(B, S, D)


def kernel(x_prompt, x_sample, w_in, w_gate_f, b_gate_f, w_gate_b, b_gate_b, sinks, gla_norm_g, w_out, ln1_g, ln1_b,
           w_router, b_router, w_gate_up, b_gate_up, w_down, b_down, ln2_g, ln2_b):
    l = 0
    wr = w_router[l]
    wr_hi = wr.astype(BF16)
    p = dict(
        w_in=w_in[l].astype(BF16),
        wgf=w_gate_f[l].astype(BF16), bgf=b_gate_f[l].reshape(1, -1),
        wgb=w_gate_b[l].astype(BF16), bgb=b_gate_b[l].reshape(1, -1),
        sinks=sinks[l], gng=gla_norm_g[l].reshape(1, -1), w_out=w_out[l].astype(BF16),
        g1=ln1_g[l].reshape(1, -1), b1=ln1_b[l].reshape(1, -1),
        wr_hi=wr_hi, wr_lo=(wr - wr_hi.astype(F32)).astype(BF16), br=b_router[l].reshape(1, -1),
        w_gu=w_gate_up[l].astype(BF16), b_gu=b_gate_up[l].reshape(N_EXPERTS, 1, -1),
        w_dn=w_down[l].astype(BF16), b_dn=b_down[l].reshape(N_EXPERTS, 1, -1),
        g2=ln2_g[l].reshape(1, -1), b2=ln2_b[l].reshape(1, -1),
    )
    return (_run_trunk(x_prompt, p), _run_trunk(x_sample, p))
```

```python
import functools

import jax
import jax.numpy as jnp
import numpy as np
from jax import lax
from jax.experimental import pallas as pl
from jax.experimental.pallas import tpu as pltpu

D_MODEL = 1024
DEPTH = 1
ATT_HEADS = 8
ATT_KV_HEADS = 2
ATT_HEAD_DIM = 64
ATT_GROUP = ATT_HEADS // ATT_KV_HEADS
WINDOW = 128
ATT_BLOCK = 128
ROPE_THETA = 10000.0
GLA_HEADS = 4
GLA_DK = 64
GLA_DV = 128
GLA_GATE_RANK = 16
GLA_TAU = 16.0
N_EXPERTS = 32
TOP_K = 4
D_FF = 1024
SWIGLU_LIMIT = 7.0
SWIGLU_ALPHA = 1.702
MOE_BLOCK = 128
LN_EPS = 1e-5
DEEPNORM_ALPHA = (2 * DEPTH) ** 0.25

ATT_Q_W = ATT_HEADS * ATT_HEAD_DIM
ATT_KV_W = ATT_KV_HEADS * ATT_HEAD_DIM
GLA_K_W = GLA_HEADS * GLA_DK
GLA_V_W = GLA_HEADS * GLA_DV
OFF_AQ = 0
OFF_AK = OFF_AQ + ATT_Q_W
OFF_AV = OFF_AK + ATT_KV_W
OFF_GQ = OFF_AV + ATT_KV_W
OFF_GK = OFF_GQ + GLA_K_W
OFF_GV = OFF_GK + GLA_K_W
OFF_GR = OFF_GV + GLA_V_W
OFF_Z = OFF_GR + GLA_V_W
IN_WIDTH = OFF_Z + 2 * GLA_GATE_RANK

LANES = 128
PROJ_TILE = 512
GLA_CHUNK = 128
GLA_FAST_MAX_DECAY = 60.0
COMBINE_TILE = 256
NEG_BIG = -1e30
VMEM_LIMIT = 48 * 1024 * 1024

BF16 = jnp.bfloat16
F32 = jnp.float32


def _dot(a, b):
    return jnp.dot(a, b, preferred_element_type=F32)


def _dot_nt(a, b):
    return lax.dot_general(a, b, (((1,), (1,)), ((), ())), preferred_element_type=F32)


def _dot_tn(a, b):
    return lax.dot_general(a, b, (((0,), (0,)), ((), ())), preferred_element_type=F32)


def _split_bf16(x):
    hi = x.astype(BF16)
    lo = (x - hi.astype(F32)).astype(BF16)
    return hi, lo


TOKEN_TILE_ROWS = D_MODEL // LANES


def _token_tile_spec(tokens, index_map=lambda i: (i, 0)):
    return pl.BlockSpec((tokens * TOKEN_TILE_ROWS, LANES), index_map)


def _load_token_tiles(ref, tokens):
    return jnp.concatenate([ref[pl.ds(c, tokens, stride=TOKEN_TILE_ROWS), :] for c in range(TOKEN_TILE_ROWS)],
                           axis=1)


def _store_token_tiles(ref, x):
    for c in range(TOKEN_TILE_ROWS):
        ref[pl.ds(c, x.shape[0], stride=TOKEN_TILE_ROWS), :] = x[:, c * LANES:(c + 1) * LANES]


def _in_proj_kernel(x_ref, w_ref, wgf_ref, bgf_ref, wgb_ref, bgb_ref, cos_ref, sin_ref,
                    aq_ref, ak_ref, av_ref, gq_ref, gk_ref, gv_ref, gr_ref, lgf_ref, lgb_ref):
    xb = x_ref[...].astype(BF16)
    cos = cos_ref[...]
    sin = sin_ref[...]
    lane = lax.broadcasted_iota(jnp.int32, (x_ref.shape[0], LANES), 1)
    first_half = (lane % ATT_HEAD_DIM) < (ATT_HEAD_DIM // 2)

    def rope(t):
        swapped = jnp.where(first_half, pltpu.roll(t, LANES - ATT_HEAD_DIM // 2, 1),
                            pltpu.roll(t, ATT_HEAD_DIM // 2, 1))
        return t * cos + swapped * sin

    def proj(lo, width):
        return _dot(xb, w_ref[:, lo:lo + width])

    q_scale = ATT_HEAD_DIM ** -0.5
    t = proj(OFF_AQ, ATT_Q_W)
    for c in range(ATT_Q_W // LANES):
        cs = slice(c * LANES, (c + 1) * LANES)
        aq_ref[:, cs] = (rope(t[:, cs]) * q_scale).astype(aq_ref.dtype)
    kv = proj(OFF_AK, 2 * ATT_KV_W)
    ak_ref[...] = rope(kv[:, :ATT_KV_W]).astype(ak_ref.dtype)
    av_ref[...] = kv[:, ATT_KV_W:].astype(av_ref.dtype)
    gq_ref[...] = proj(OFF_GQ, GLA_K_W) * (GLA_DK ** -0.5)
    gk_ref[...] = proj(OFF_GK, GLA_K_W)
    gv_ref[...] = proj(OFF_GV, GLA_V_W).astype(gv_ref.dtype)
    grz = proj(OFF_GR, GLA_V_W + 2 * GLA_GATE_RANK)
    gr_ref[...] = grz[:, :GLA_V_W].astype(gr_ref.dtype)
    z = grz[:, GLA_V_W:]

    def log_decay(zr, wg_ref, bg_ref):
        pre = _dot(zr.astype(BF16), wg_ref[...]) + bg_ref[...]
        log_sig = jnp.minimum(pre, 0.0) - jnp.log(1.0 + jnp.exp(-jnp.abs(pre)))
        return log_sig * (1.0 / GLA_TAU)

    lgf_ref[...] = log_decay(z[:, :GLA_GATE_RANK], wgf_ref, bgf_ref)
    lgb_ref[...] = log_decay(z[:, GLA_GATE_RANK:], wgb_ref, bgb_ref)


def _in_proj(x2, w_in, wgf, bgf, wgb, bgb, cos_t, sin_t, seq):
    T = x2.shape[0]
    tm = PROJ_TILE
    assert T % tm == 0 and seq % tm == 0
    pos_tiles = seq // tm
    row = lambda w: pl.BlockSpec((tm, w), lambda i: (i, 0))
    full = lambda a: pl.BlockSpec(a.shape, lambda i: (0,) * a.ndim)
    pos = pl.BlockSpec((tm, LANES), lambda i: (i % pos_tiles, 0))
    out_w = [(ATT_Q_W, BF16), (ATT_KV_W, BF16), (ATT_KV_W, BF16), (GLA_K_W, F32), (GLA_K_W, F32),
             (GLA_V_W, BF16), (GLA_V_W, BF16), (GLA_K_W, F32), (GLA_K_W, F32)]
    return pl.pallas_call(
        _in_proj_kernel,
        grid=(T // tm,),
        in_specs=[row(D_MODEL), full(w_in), full(wgf), full(bgf), full(wgb), full(bgb), pos, pos],
        out_specs=[row(w) for w, _ in out_w],
        out_shape=[jax.ShapeDtypeStruct((T, w), dt) for w, dt in out_w],
        compiler_params=pltpu.CompilerParams(dimension_semantics=("arbitrary",), vmem_limit_bytes=VMEM_LIMIT),
        name="in_proj",
    )(x2, w_in, wgf, bgf, wgb, bgb, cos_t, sin_t)


def _attention_kernel(sinks_ref, q_ref, kp_ref, kc_ref, kn_ref, vp_ref, vc_ref, vn_ref, o_ref, *, seq):
    n = pl.program_id(1)
    R = ATT_BLOCK
    D = ATT_HEAD_DIM
    kcat = jnp.concatenate([kp_ref[...], kc_ref[...], kn_ref[...]], axis=0)
    vcat = jnp.concatenate([vp_ref[...], vc_ref[...], vn_ref[...]], axis=0)
    qi = lax.broadcasted_iota(jnp.int32, (R, 3 * R), 0)
    kj = lax.broadcasted_iota(jnp.int32, (R, 3 * R), 1)
    rel = kj - R - qi
    key_pos = n * R - R + kj
    valid = (jnp.abs(rel) <= WINDOW) & (key_pos >= 0) & (key_pos < seq)
    assert ATT_KV_HEADS * D == LANES
    v_lane = lax.broadcasted_iota(jnp.int32, vcat.shape, 1)
    v_aug = [jnp.where((v_lane < D) == (kv == 0), vcat, jnp.ones_like(vcat)) for kv in range(ATT_KV_HEADS)]

    scores = []
    for h in range(ATT_HEADS):
        kv = h // ATT_GROUP
        s = _dot_nt(q_ref[:, h * D:(h + 1) * D], kcat[:, kv * D:(kv + 1) * D])
        scores.append(jnp.where(valid, s, NEG_BIG))
    probs, sink_terms = [], []
    for h in range(ATT_HEADS):
        sink = sinks_ref[h]
        m = jnp.maximum(jnp.max(scores[h], axis=1, keepdims=True), sink)
        probs.append(jnp.exp(scores[h] - m).astype(BF16))
        sink_terms.append(jnp.exp(sink - m))
    for h in range(ATT_HEADS):
        kv = h // ATT_GROUP
        res = _dot(probs[h], v_aug[kv])
        val = res[:, kv * D:(kv + 1) * D]
        den = res[:, (1 - kv) * D:(1 - kv) * D + 1] + sink_terms[h]
        o_ref[:, h * D:(h + 1) * D] = (val / den).astype(o_ref.dtype)


def _attention(aq, ak, av, sinks, batch, seq):
    R = ATT_BLOCK
    nb = seq // R
    qspec = pl.BlockSpec((R, ATT_Q_W), lambda b, n: (b * nb + n, 0))
    prev = pl.BlockSpec((R, ATT_KV_W), lambda b, n: (b * nb + jnp.maximum(n - 1, 0), 0))
    cur = pl.BlockSpec((R, ATT_KV_W), lambda b, n: (b * nb + n, 0))
    nxt = pl.BlockSpec((R, ATT_KV_W), lambda b, n: (b * nb + jnp.minimum(n + 1, nb - 1), 0))
    return pl.pallas_call(
        functools.partial(_attention_kernel, seq=seq),
        grid=(batch, nb),
        in_specs=[pl.BlockSpec(memory_space=pltpu.SMEM), qspec, prev, cur, nxt, prev, cur, nxt],
        out_specs=qspec,
        out_shape=jax.ShapeDtypeStruct(aq.shape, BF16),
        compiler_params=pltpu.CompilerParams(dimension_semantics=("arbitrary", "arbitrary"),
                                             vmem_limit_bytes=VMEM_LIMIT),
        name="swa_attention",
    )(sinks, aq, ak, ak, ak, av, av, av)


_GLA_HEAD_SLICES = [(slice(h * GLA_DK, (h + 1) * GLA_DK), slice(h * GLA_DV, (h + 1) * GLA_DV)) for h in range(GLA_HEADS)]


def _gla_decays(q_ref, k_ref, lg_ref, reverse):
    C = q_ref.shape[0]
    ri = lax.broadcasted_iota(jnp.int32, (C, C), 0)
    cj = lax.broadcasted_iota(jnp.int32, (C, C), 1)
    pair = (cj >= ri) if reverse else (cj <= ri)
    tri = pair.astype(BF16)
    lg_hi, lg_lo = _split_bf16(lg_ref[...])
    bc = _dot(tri, lg_hi) + _dot(tri, lg_lo)
    end = 0 if reverse else C - 1
    b_end = bc[end:end + 1, :]
    qt = (q_ref[...] * jnp.exp(bc)).astype(BF16)
    k_to_end = (k_ref[...] * jnp.exp(b_end - bc)).astype(BF16)
    return pair, bc, b_end, qt, k_to_end


def _gla_exact_intra(q_ref, k_ref, v_ref, bc, oi_ref, bc_ref, vf_ref, reverse):
    C = q_ref.shape[0]
    q = q_ref[...]
    bc_ref[...] = bc
    vf_ref[...] = v_ref[...].astype(F32)
    oi_ref[...] = jnp.zeros_like(oi_ref)
    rows = lax.broadcasted_iota(jnp.int32, (C, 1), 0)

    def body(j, carry):
        kj = k_ref[pl.ds(j, 1), :]
        bj = bc_ref[pl.ds(j, 1), :]
        vj = vf_ref[pl.ds(j, 1), :]
        live = (rows <= j) if reverse else (rows >= j)
        t = jnp.where(live, q * kj * jnp.exp(jnp.minimum(bc - bj, 0.0)), 0.0)
        for ks, vs in _GLA_HEAD_SLICES:
            a = jnp.sum(t[:, ks], axis=1, keepdims=True)
            oi_ref[:, vs] += a * vj[:, vs]
        return carry

    lax.fori_loop(0, C, body, 0)
    return [oi_ref[:, vs] for _, vs in _GLA_HEAD_SLICES]


def _gla_kernel(qf_ref, kf_ref, vf_ref, lgf_ref, qb_ref, kb_ref, vb_ref, lgb_ref, of_ref, ob_ref,
                stf_ref, stb_ref, oi_ref, bc_ref, vv_ref):
    @pl.when(pl.program_id(1) == 0)
    def _():
        stf_ref[...] = jnp.zeros_like(stf_ref)
        stb_ref[...] = jnp.zeros_like(stb_ref)

    dirs = [(qf_ref, kf_ref, vf_ref, lgf_ref, of_ref, stf_ref, False),
            (qb_ref, kb_ref, vb_ref, lgb_ref, ob_ref, stb_ref, True)]

    def both(fast):
        dec = [_gla_decays(q_ref, k_ref, lg_ref, rev) for q_ref, k_ref, _, lg_ref, _, _, rev in dirs]
        if fast:
            scores = []
            for (_, k_ref, _, _, _, _, _), (_, bc, _, qt, _) in zip(dirs, dec):
                kt = (k_ref[...] * jnp.exp(-bc)).astype(BF16)
                scores.append([_dot_nt(qt[:, ks], kt[:, ks]) for ks, _ in _GLA_HEAD_SLICES])
        inter, new_state = [], []
        for (_, _, v_ref, _, _, st_ref, _), (_, _, b_end, qt, k_to_end) in zip(dirs, dec):
            st = st_ref[...]
            st_b = st.astype(BF16)
            inter.append([_dot_nt(qt[:, ks], st_b[:, ks]) for ks, _ in _GLA_HEAD_SLICES])
            cols = [_dot_tn(v_ref[:, vs], k_to_end[:, ks]) for ks, vs in _GLA_HEAD_SLICES]
            new_state.append(st * jnp.exp(b_end) + jnp.concatenate(cols, axis=1))
        for d, ((q_ref, k_ref, v_ref, _, o_ref, st_ref, rev), (pair, bc, _, _, _)) in enumerate(zip(dirs, dec)):
            if fast:
                intra = [_dot(jnp.where(pair, a, 0.0).astype(BF16), v_ref[:, vs])
                         for a, (_, vs) in zip(scores[d], _GLA_HEAD_SLICES)]
            else:
                intra = _gla_exact_intra(q_ref, k_ref, v_ref, bc, oi_ref, bc_ref, vv_ref, rev)
            for (_, vs), o_inter, o_intra in zip(_GLA_HEAD_SLICES, inter[d], intra):
                o_ref[:, vs] = o_inter + o_intra
            st_ref[...] = new_state[d]

    chunk_decay = jnp.minimum(jnp.min(jnp.sum(lgf_ref[...], axis=0, keepdims=True)),
                              jnp.min(jnp.sum(lgb_ref[...], axis=0, keepdims=True)))
    fast_ok = chunk_decay >= -GLA_FAST_MAX_DECAY
    pl.when(fast_ok)(lambda: both(True))
    pl.when(jnp.logical_not(fast_ok))(lambda: both(False))


def _gla(gq, gk, gv, lgf, lgb, batch, seq):
    C = GLA_CHUNK
    nc = seq // C
    T = gq.shape[0]
    fwd = lambda w: pl.BlockSpec((C, w), lambda b, n: (b * nc + n, 0))
    bwd = lambda w: pl.BlockSpec((C, w), lambda b, n: (b * nc + nc - 1 - n, 0))
    return pl.pallas_call(
        _gla_kernel,
        grid=(batch, nc),
        in_specs=[fwd(GLA_K_W), fwd(GLA_K_W), fwd(GLA_V_W), fwd(GLA_K_W),
                  bwd(GLA_K_W), bwd(GLA_K_W), bwd(GLA_V_W), bwd(GLA_K_W)],
        out_specs=[fwd(GLA_V_W), bwd(GLA_V_W)],
        out_shape=[jax.ShapeDtypeStruct((T, GLA_V_W), F32)] * 2,
        scratch_shapes=[pltpu.VMEM((GLA_DV, GLA_K_W), F32), pltpu.VMEM((GLA_DV, GLA_K_W), F32),
                        pltpu.VMEM((C, GLA_V_W), F32), pltpu.VMEM((C, GLA_K_W), F32),
                        pltpu.VMEM((C, GLA_V_W), F32)],
        compiler_params=pltpu.CompilerParams(dimension_semantics=("arbitrary", "arbitrary"),
                                             vmem_limit_bytes=VMEM_LIMIT),
        name="gla_bidir",
    )(gq, gk, gv, lgf, gq, gk, gv, lgb)


def _layer_norm(y, g, b):
    yc = y - jnp.mean(y, axis=-1, keepdims=True)
    var = jnp.mean(yc * yc, axis=-1, keepdims=True)
    return yc * lax.rsqrt(var + LN_EPS) * g + b


def _post_mixer_kernel(of_ref, ob_ref, gr_ref, att_ref, x_ref, gng_ref, wo_ref, g1_ref, b1_ref,
                       wr_ref, br_ref, x1_ref, ei_ref, gw_ref):
    tm = x_ref.shape[0]
    mixed = [att_ref[...]]
    for h in range(GLA_HEADS):
        vs = slice(h * GLA_DV, (h + 1) * GLA_DV)
        o = of_ref[:, vs] + ob_ref[:, vs]
        o = o * lax.rsqrt(jnp.mean(o * o, axis=-1, keepdims=True) + LN_EPS) * gng_ref[...]
        r = gr_ref[:, vs].astype(F32)
        mixed.append((o * (r / (1.0 + jnp.exp(-r)))).astype(BF16))
    m = _dot(jnp.concatenate(mixed, axis=1), wo_ref[...])
    x1 = _layer_norm(DEEPNORM_ALPHA * x_ref[...] + m, g1_ref[...], b1_ref[...])
    _store_token_tiles(x1_ref, x1)

    xh, xl = _split_bf16(x1)
    hi_lo = _dot(xh, wr_ref[...])
    logits = hi_lo[:, :N_EXPERTS] + hi_lo[:, N_EXPERTS:] + _dot(xl, wr_ref[:, :N_EXPERTS]) + br_ref[...]
    lane = lax.broadcasted_iota(jnp.int32, (tm, N_EXPERTS), 1).astype(F32)
    out_lane = lax.broadcasted_iota(jnp.int32, (tm, LANES), 1)
    vals = logits
    top_v = []
    e_out = jnp.zeros((tm, LANES), jnp.int32)
    for kk in range(TOP_K):
        mx = jnp.max(vals, axis=1, keepdims=True)
        idx = jnp.min(jnp.where(vals == mx, lane, float(N_EXPERTS)), axis=1, keepdims=True)
        top_v.append(mx)
        e_out = jnp.where(out_lane == kk, idx.astype(jnp.int32), e_out)
        vals = jnp.where(lane == idx, -jnp.inf, vals)
    ex = [jnp.exp(v - top_v[0]) for v in top_v]
    den = ex[0] + ex[1] + ex[2] + ex[3]
    g_out = jnp.zeros((tm, LANES), F32)
    for kk in range(TOP_K):
        g_out = jnp.where(out_lane == kk, ex[kk] / den, g_out)
    ei_ref[...] = e_out
    gw_ref[...] = g_out


def _post_mixer(o_f, o_b, gr, att, x2, gng, w_out, g1, b1, wr_cat, br):
    T = x2.shape[0]
    tm = PROJ_TILE
    row = lambda w: pl.BlockSpec((tm, w), lambda i: (i, 0))
    full = lambda a: pl.BlockSpec(a.shape, lambda i: (0,) * a.ndim)
    return pl.pallas_call(
        _post_mixer_kernel,
        grid=(T // tm,),
        in_specs=[row(GLA_V_W), row(GLA_V_W), row(GLA_V_W), row(ATT_Q_W), row(D_MODEL), full(gng), full(w_out),
                  full(g1), full(b1), full(wr_cat), full(br)],
        out_specs=[_token_tile_spec(tm), row(LANES), row(LANES)],
        out_shape=[jax.ShapeDtypeStruct((T * TOKEN_TILE_ROWS, LANES), F32), jax.ShapeDtypeStruct((T, LANES), jnp.int32),
                   jax.ShapeDtypeStruct((T, LANES), F32)],
        compiler_params=pltpu.CompilerParams(dimension_semantics=("arbitrary",), vmem_limit_bytes=VMEM_LIMIT),
        name="post_mixer_router",
    )(o_f, o_b, gr, att, x2, gng, w_out, g1, b1, wr_cat, br)


MOE_SLOTS = 3
MOE_FF_CHUNK = 256


def _moe_kernel(be_ref, dprev_ref, scur_ref, snext_ref, snext2_ref, x_hbm, wgu_ref, bgu_ref, wd_ref, bd_ref, y_hbm,
                *scratch, plane):
    del be_ref
    R = MOE_BLOCK
    TR = TOKEN_TILE_ROWS
    i = pl.program_id(0)
    xbuf, ybuf = scratch[:MOE_SLOTS], scratch[MOE_SLOTS:2 * MOE_SLOTS]
    gsem, ssem = scratch[2 * MOE_SLOTS:]

    def tile(ref, row):
        return ref.at[pl.ds(pl.multiple_of(row * TR, TR), TR), :]

    def gather_row(idx_ref, r, s):
        tok = idx_ref[0, 0, r] & (plane - 1)
        return pltpu.make_async_copy(tile(x_hbm, tok), tile(xbuf[s], r), gsem.at[s])

    def scatter_row(idx_ref, r, s):
        return pltpu.make_async_copy(tile(ybuf[s], r), tile(y_hbm, idx_ref[0, 0, r]), ssem.at[s])

    def wait_gather(s):
        pltpu.make_async_copy(x_hbm.at[pl.ds(0, R * TR), :], xbuf[s], gsem.at[s]).wait()

    def wait_scatter(s):
        pltpu.make_async_copy(ybuf[s], y_hbm.at[pl.ds(0, R * TR), :], ssem.at[s]).wait()

    @pl.when(i == 0)
    def _():
        def issue(r, carry):
            gather_row(scur_ref, r, 0).start()
            gather_row(snext_ref, r, 1).start()
            return carry

        lax.fori_loop(0, R, issue, 0)
        ybuf[MOE_SLOTS - 1][...] = jnp.zeros((R * TR, LANES), F32)

    def step(slot):
        ahead = (slot + 2) % MOE_SLOTS
        wait_gather(slot)

        @pl.when(i >= MOE_SLOTS - 1)
        def _():
            wait_scatter(slot)

        for r in range(R):
            gather_row(snext2_ref, r, ahead).start()
            scatter_row(dprev_ref, r, ahead).start(priority=1)

        xb = _load_token_tiles(xbuf[slot], R).astype(BF16)
        acts = []
        for c in range(D_FF // MOE_FF_CHUNK):
            gs = slice(c * MOE_FF_CHUNK, (c + 1) * MOE_FF_CHUNK)
            us = slice(D_FF + c * MOE_FF_CHUNK, D_FF + (c + 1) * MOE_FF_CHUNK)
            gate = jnp.minimum(_dot(xb, wgu_ref[0, :, gs]) + bgu_ref[0, :, gs], SWIGLU_LIMIT)
            up = jnp.clip(_dot(xb, wgu_ref[0, :, us]) + bgu_ref[0, :, us], -SWIGLU_LIMIT, SWIGLU_LIMIT)
            acts.append(((up + 1.0) * gate / (1.0 + jnp.exp(-SWIGLU_ALPHA * gate))).astype(BF16))
        act = jnp.concatenate(acts, axis=1)
        _store_token_tiles(ybuf[slot], _dot(act, wd_ref[0]) + bd_ref[0])

        @pl.when(i == pl.num_programs(0) - 1)
        def _():
            wait_gather((slot + 1) % MOE_SLOTS)
            wait_gather(ahead)
            wait_scatter((slot + 1) % MOE_SLOTS)
            wait_scatter(ahead)

    for slot in range(MOE_SLOTS):
        pl.when(i % MOE_SLOTS == slot)(functools.partial(step, slot))


def _moe(x1t, block_e, row_code, w_gu, b_gu, w_dn, b_dn, plane):
    R = MOE_BLOCK
    TR = TOKEN_TILE_ROWS
    steps = block_e.shape[0]
    smem_blk = lambda off: pl.BlockSpec((1, 1, R), lambda i, be: (i + off, 0, 0), memory_space=pltpu.SMEM)
    grid_spec = pltpu.PrefetchScalarGridSpec(
        num_scalar_prefetch=1,
        grid=(steps,),
        in_specs=[
            smem_blk(0), smem_blk(1), smem_blk(2), smem_blk(3),
            pl.BlockSpec(memory_space=pl.ANY),
            pl.BlockSpec((1, D_MODEL, 2 * D_FF), lambda i, be: (be[i], 0, 0)),
            pl.BlockSpec((1, 1, 2 * D_FF), lambda i, be: (be[i], 0, 0)),
            pl.BlockSpec((1, D_FF, D_MODEL), lambda i, be: (be[i], 0, 0)),
            pl.BlockSpec((1, 1, D_MODEL), lambda i, be: (be[i], 0, 0)),
        ],
        out_specs=pl.BlockSpec(memory_space=pl.ANY),
        scratch_shapes=[pltpu.VMEM((R * TR, LANES), F32)] * (2 * MOE_SLOTS)
        + [pltpu.SemaphoreType.DMA((MOE_SLOTS,)), pltpu.SemaphoreType.DMA((MOE_SLOTS,))],
    )
    return pl.pallas_call(
        functools.partial(_moe_kernel, plane=plane),
        grid_spec=grid_spec,
        out_shape=jax.ShapeDtypeStruct(((TOP_K * plane + 2 * R) * TR, LANES), F32),
        compiler_params=pltpu.CompilerParams(dimension_semantics=("arbitrary",), vmem_limit_bytes=VMEM_LIMIT),
        name="moe_experts",
    )(block_e, row_code, row_code, row_code, row_code, x1t, w_gu, b_gu, w_dn, b_dn)


def _combine_kernel(y0_ref, y1_ref, y2_ref, y3_ref, gw_ref, x1_ref, g2_ref, b2_ref, out_ref):
    tm = out_ref.shape[0]
    gw = gw_ref[...]
    f = gw[:, 0:1] * _load_token_tiles(y0_ref, tm)
    for kk, y_ref in enumerate((y1_ref, y2_ref, y3_ref), start=1):
        f = f + gw[:, kk:kk + 1] * _load_token_tiles(y_ref, tm)
    out_ref[...] = _layer_norm(DEEPNORM_ALPHA * _load_token_tiles(x1_ref, tm) + f, g2_ref[...], b2_ref[...])


def _combine(y4, gw, x1t, g2, b2, plane):
    T = gw.shape[0]
    tm = COMBINE_TILE
    plane_tiles = plane // tm
    row = lambda w: pl.BlockSpec((tm, w), lambda i: (i, 0))
    full = lambda a: pl.BlockSpec(a.shape, lambda i: (0,) * a.ndim)
    slot_plane = lambda kk: _token_tile_spec(tm, lambda i: (kk * plane_tiles + i, 0))
    return pl.pallas_call(
        _combine_kernel,
        grid=(T // tm,),
        in_specs=[slot_plane(0), slot_plane(1), slot_plane(2), slot_plane(3), row(LANES), _token_tile_spec(tm),
                  full(g2), full(b2)],
        out_specs=row(D_MODEL),
        out_shape=jax.ShapeDtypeStruct((T, D_MODEL), F32),
        compiler_params=pltpu.CompilerParams(dimension_semantics=("arbitrary",), vmem_limit_bytes=VMEM_LIMIT),
        name="moe_combine_ln",
    )(y4, y4, y4, y4, gw, x1t, g2, b2)


def _routing_tables(e_idx, plane):
    T = e_idx.shape[0]
    R, E = MOE_BLOCK, N_EXPERTS
    M = T * TOP_K
    m_bits = (M - 1).bit_length()
    assert (E << m_bits) < 2 ** 31
    flat_e = e_idx.reshape(M)
    m_sorted = jnp.sort((flat_e << m_bits) + jnp.arange(M, dtype=jnp.int32)) & ((1 << m_bits) - 1)
    code_sorted = (m_sorted % TOP_K) * plane + m_sorted // TOP_K
    counts = jnp.sum((flat_e[:, None] == jnp.arange(E, dtype=jnp.int32)[None, :]).astype(jnp.int32), axis=0)
    padded = ((counts + R - 1) // R) * R
    ends_pad = jnp.cumsum(padded)
    starts_pad = ends_pad - padded
    starts = jnp.cumsum(counts) - counts
    n_blocks = M // R + E
    block_start = jnp.arange(n_blocks + 1, dtype=jnp.int32) * R
    block_e = jnp.minimum(jnp.sum(block_start[:, None] >= ends_pad[None, :], axis=1), E - 1).astype(jnp.int32)
    be = block_e[:n_blocks]
    in_row = jnp.arange(R, dtype=jnp.int32)[None, :]
    local = (block_start[:n_blocks] - starts_pad[be])[:, None] + in_row
    valid = local < counts[be][:, None]
    src = jnp.minimum(starts[be][:, None] + local, M - 1)
    parity = (jnp.arange(n_blocks + 4, dtype=jnp.int32) % 2)[:, None]
    spare = TOP_K * plane + parity * R + in_row
    body = jnp.where(valid, code_sorted[src], spare[1:n_blocks + 1])
    row_code = jnp.concatenate([spare[:1], body, spare[n_blocks + 1:]], axis=0)
    return row_code.reshape(n_blocks + 4, 1, R), block_e


def _rope_tables(seq):
    half = ATT_HEAD_DIM // 2
    inv_freq = ROPE_THETA ** (-jnp.arange(half, dtype=F32) * 2.0 / ATT_HEAD_DIM)
    ang = jnp.arange(seq, dtype=F32)[:, None] * inv_freq[None, :]
    cos, sin = jnp.cos(ang), jnp.sin(ang)
    reps = LANES // ATT_HEAD_DIM
    cos_t = jnp.tile(jnp.concatenate([cos, cos], axis=1), (1, reps))
    sin_t = jnp.tile(jnp.concatenate([-sin, sin], axis=1), (1, reps))
    return cos_t, sin_t


def _run_trunk(x, p):
    B, S, D = x.shape
    T = B * S
    plane = max(1 << (T - 1).bit_length(), COMBINE_TILE)
    assert T % COMBINE_TILE == 0 and T >= 2 * MOE_BLOCK
    x2 = x.reshape(T, D)
    cos_t, sin_t = _rope_tables(S)
    aq, ak, av, gq, gk, gv, gr, lgf, lgb = _in_proj(x2, p["w_in"], p["wgf"], p["bgf"], p["wgb"], p["bgb"],
                                                    cos_t, sin_t, S)
    att = _attention(aq, ak, av, p["sinks"], B, S)
    o_f, o_b = _gla(gq, gk, gv, lgf, lgb, B, S)
    x1t, e_pad, g_pad = _post_mixer(o_f, o_b, gr, att, x2, p["gng"], p["w_out"], p["g1"], p["b1"],
                                    p["wr_cat"], p["br"])
    row_code, block_e = _routing_tables(e_pad[:, :TOP_K], plane)
    y4 = _moe(x1t, block_e, row_code, p["w_gu"], p["b_gu"], p["w_dn"], p["b_dn"], plane)
    y = _combine(y4, g_pad, x1t, p["g2"], p["b2"], plane)
    return y.reshape(B, S, D)


def kernel(x_prompt, x_sample, w_in, w_gate_f, b_gate_f, w_gate_b, b_gate_b, sinks, gla_norm_g, w_out, ln1_g, ln1_b,
           w_router, b_router, w_gate_up, b_gate_up, w_down, b_down, ln2_g, ln2_b):
    l = 0
    wr = w_router[l]
    wr_hi = wr.astype(BF16)
    p = dict(
        w_in=w_in[l].astype(BF16),
        wgf=w_gate_f[l].astype(BF16), bgf=b_gate_f[l].reshape(1, -1),
        wgb=w_gate_b[l].astype(BF16), bgb=b_gate_b[l].reshape(1, -1),
        sinks=sinks[l], gng=gla_norm_g[l].reshape(1, -1), w_out=w_out[l].astype(BF16),
        g1=ln1_g[l].reshape(1, -1), b1=ln1_b[l].reshape(1, -1),
        wr_cat=jnp.concatenate([wr_hi, (wr - wr_hi.astype(F32)).astype(BF16)], axis=1), br=b_router[l].reshape(1, -1),
        w_gu=w_gate_up[l].astype(BF16), b_gu=b_gate_up[l].reshape(N_EXPERTS, 1, -1),
        w_dn=w_down[l].astype(BF16), b_dn=b_down[l].reshape(N_EXPERTS, 1, -1),
        g2=ln2_g[l].reshape(1, -1), b2=ln2_b[l].reshape(1, -1),
    )
    return (_run_trunk(x_prompt, p), _run_trunk(x_sample, p))
```

```python
import functools

import jax
import jax.numpy as jnp
import numpy as np
from jax import lax
from jax.experimental import pallas as pl
from jax.experimental.pallas import tpu as pltpu

D_MODEL = 1024
DEPTH = 1
ATT_HEADS = 8
ATT_KV_HEADS = 2
ATT_HEAD_DIM = 64
ATT_GROUP = ATT_HEADS // ATT_KV_HEADS
WINDOW = 128
ATT_BLOCK = 128
ROPE_THETA = 10000.0
GLA_HEADS = 4
GLA_DK = 64
GLA_DV = 128
GLA_GATE_RANK = 16
GLA_TAU = 16.0
N_EXPERTS = 32
TOP_K = 4
D_FF = 1024
SWIGLU_LIMIT = 7.0
SWIGLU_ALPHA = 1.702
MOE_BLOCK = 128
LN_EPS = 1e-5
DEEPNORM_ALPHA = (2 * DEPTH) ** 0.25

ATT_Q_W = ATT_HEADS * ATT_HEAD_DIM
ATT_KV_W = ATT_KV_HEADS * ATT_HEAD_DIM
GLA_K_W = GLA_HEADS * GLA_DK
GLA_V_W = GLA_HEADS * GLA_DV
OFF_AQ = 0
OFF_AK = OFF_AQ + ATT_Q_W
OFF_AV = OFF_AK + ATT_KV_W
OFF_GQ = OFF_AV + ATT_KV_W
OFF_GK = OFF_GQ + GLA_K_W
OFF_GV = OFF_GK + GLA_K_W
OFF_GR = OFF_GV + GLA_V_W
OFF_Z = OFF_GR + GLA_V_W
IN_WIDTH = OFF_Z + 2 * GLA_GATE_RANK

LANES = 128
PROJ_TILE = 512
GLA_CHUNK = 128
GLA_FAST_MAX_DECAY = 60.0
COMBINE_TILE = 256
NEG_BIG = -1e30
VMEM_LIMIT = 56 * 1024 * 1024

BF16 = jnp.bfloat16
F32 = jnp.float32


def _dot(a, b):
    return jnp.dot(a, b, preferred_element_type=F32)


def _dot_nt(a, b):
    return lax.dot_general(a, b, (((1,), (1,)), ((), ())), preferred_element_type=F32)


def _dot_tn(a, b):
    return lax.dot_general(a, b, (((0,), (0,)), ((), ())), preferred_element_type=F32)


def _split_bf16(x):
    hi = x.astype(BF16)
    lo = (x - hi.astype(F32)).astype(BF16)
    return hi, lo


TOKEN_TILE_ROWS = D_MODEL // LANES


def _token_tile_spec(tokens, index_map=lambda i: (i, 0)):
    return pl.BlockSpec((tokens * TOKEN_TILE_ROWS, LANES), index_map)


def _load_token_tiles(ref, tokens):
    return jnp.concatenate([ref[pl.ds(c, tokens, stride=TOKEN_TILE_ROWS), :] for c in range(TOKEN_TILE_ROWS)],
                           axis=1)


def _store_token_tiles(ref, x):
    for c in range(TOKEN_TILE_ROWS):
        ref[pl.ds(c, x.shape[0], stride=TOKEN_TILE_ROWS), :] = x[:, c * LANES:(c + 1) * LANES]


def _in_proj_kernel(x_ref, w_ref, wgf_ref, bgf_ref, wgb_ref, bgb_ref, cos_ref, sin_ref,
                    aq_ref, ak_ref, av_ref, gq_ref, gk_ref, gv_ref, gr_ref, lgf_ref, lgb_ref):
    xb = x_ref[...].astype(BF16)
    cos = cos_ref[...]
    sin = sin_ref[...]
    lane = lax.broadcasted_iota(jnp.int32, (x_ref.shape[0], LANES), 1)
    first_half = (lane % ATT_HEAD_DIM) < (ATT_HEAD_DIM // 2)

    def rope(t):
        swapped = jnp.where(first_half, pltpu.roll(t, LANES - ATT_HEAD_DIM // 2, 1),
                            pltpu.roll(t, ATT_HEAD_DIM // 2, 1))
        return t * cos + swapped * sin

    def proj(lo, width):
        return _dot(xb, w_ref[:, lo:lo + width])

    q_scale = ATT_HEAD_DIM ** -0.5
    t = proj(OFF_AQ, ATT_Q_W)
    for c in range(ATT_Q_W // LANES):
        cs = slice(c * LANES, (c + 1) * LANES)
        aq_ref[:, cs] = (rope(t[:, cs]) * q_scale).astype(aq_ref.dtype)
    kv = proj(OFF_AK, 2 * ATT_KV_W)
    ak_ref[...] = rope(kv[:, :ATT_KV_W]).astype(ak_ref.dtype)
    av_ref[...] = kv[:, ATT_KV_W:].astype(av_ref.dtype)
    gq_ref[...] = proj(OFF_GQ, GLA_K_W) * (GLA_DK ** -0.5)
    gk_ref[...] = proj(OFF_GK, GLA_K_W)
    gv_ref[...] = proj(OFF_GV, GLA_V_W).astype(gv_ref.dtype)
    grz = proj(OFF_GR, GLA_V_W + 2 * GLA_GATE_RANK)
    gr_ref[...] = grz[:, :GLA_V_W].astype(gr_ref.dtype)
    z = grz[:, GLA_V_W:]

    def log_decay(zr, wg_ref, bg_ref):
        pre = _dot(zr.astype(BF16), wg_ref[...]) + bg_ref[...]
        log_sig = jnp.minimum(pre, 0.0) - jnp.log(1.0 + jnp.exp(-jnp.abs(pre)))
        return log_sig * (1.0 / GLA_TAU)

    lgf_ref[...] = log_decay(z[:, :GLA_GATE_RANK], wgf_ref, bgf_ref)
    lgb_ref[...] = log_decay(z[:, GLA_GATE_RANK:], wgb_ref, bgb_ref)


def _in_proj(x2, w_in, wgf, bgf, wgb, bgb, cos_t, sin_t, seq):
    T = x2.shape[0]
    tm = PROJ_TILE
    assert T % tm == 0 and seq % tm == 0
    pos_tiles = seq // tm
    row = lambda w: pl.BlockSpec((tm, w), lambda i: (i, 0))
    full = lambda a: pl.BlockSpec(a.shape, lambda i: (0,) * a.ndim)
    pos = pl.BlockSpec((tm, LANES), lambda i: (i % pos_tiles, 0))
    out_w = [(ATT_Q_W, BF16), (ATT_KV_W, BF16), (ATT_KV_W, BF16), (GLA_K_W, F32), (GLA_K_W, F32),
             (GLA_V_W, BF16), (GLA_V_W, BF16), (GLA_K_W, F32), (GLA_K_W, F32)]
    return pl.pallas_call(
        _in_proj_kernel,
        grid=(T // tm,),
        in_specs=[row(D_MODEL), full(w_in), full(wgf), full(bgf), full(wgb), full(bgb), pos, pos],
        out_specs=[row(w) for w, _ in out_w],
        out_shape=[jax.ShapeDtypeStruct((T, w), dt) for w, dt in out_w],
        compiler_params=pltpu.CompilerParams(dimension_semantics=("arbitrary",), vmem_limit_bytes=VMEM_LIMIT),
        name="in_proj",
    )(x2, w_in, wgf, bgf, wgb, bgb, cos_t, sin_t)


def _attention_kernel(sinks_ref, q_ref, kp_ref, kc_ref, kn_ref, vp_ref, vc_ref, vn_ref, o_ref, *, seq):
    n = pl.program_id(1)
    R = ATT_BLOCK
    D = ATT_HEAD_DIM
    kcat = jnp.concatenate([kp_ref[...], kc_ref[...], kn_ref[...]], axis=0)
    vcat = jnp.concatenate([vp_ref[...], vc_ref[...], vn_ref[...]], axis=0)
    qi = lax.broadcasted_iota(jnp.int32, (R, 3 * R), 0)
    kj = lax.broadcasted_iota(jnp.int32, (R, 3 * R), 1)
    rel = kj - R - qi
    key_pos = n * R - R + kj
    valid = (jnp.abs(rel) <= WINDOW) & (key_pos >= 0) & (key_pos < seq)
    assert ATT_KV_HEADS * D == LANES
    v_lane = lax.broadcasted_iota(jnp.int32, vcat.shape, 1)
    v_aug = [jnp.where((v_lane < D) == (kv == 0), vcat, jnp.ones_like(vcat)) for kv in range(ATT_KV_HEADS)]

    scores = []
    for h in range(ATT_HEADS):
        kv = h // ATT_GROUP
        s = _dot_nt(q_ref[:, h * D:(h + 1) * D], kcat[:, kv * D:(kv + 1) * D])
        scores.append(jnp.where(valid, s, NEG_BIG))
    probs, sink_terms = [], []
    for h in range(ATT_HEADS):
        sink = sinks_ref[h]
        m = jnp.maximum(jnp.max(scores[h], axis=1, keepdims=True), sink)
        probs.append(jnp.exp(scores[h] - m).astype(BF16))
        sink_terms.append(jnp.exp(sink - m))
    for h in range(ATT_HEADS):
        kv = h // ATT_GROUP
        res = _dot(probs[h], v_aug[kv])
        val = res[:, kv * D:(kv + 1) * D]
        den = res[:, (1 - kv) * D:(1 - kv) * D + 1] + sink_terms[h]
        o_ref[:, h * D:(h + 1) * D] = (val / den).astype(o_ref.dtype)


def _attention(aq, ak, av, sinks, batch, seq):
    R = ATT_BLOCK
    nb = seq // R
    qspec = pl.BlockSpec((R, ATT_Q_W), lambda b, n: (b * nb + n, 0))
    prev = pl.BlockSpec((R, ATT_KV_W), lambda b, n: (b * nb + jnp.maximum(n - 1, 0), 0))
    cur = pl.BlockSpec((R, ATT_KV_W), lambda b, n: (b * nb + n, 0))
    nxt = pl.BlockSpec((R, ATT_KV_W), lambda b, n: (b * nb + jnp.minimum(n + 1, nb - 1), 0))
    return pl.pallas_call(
        functools.partial(_attention_kernel, seq=seq),
        grid=(batch, nb),
        in_specs=[pl.BlockSpec(memory_space=pltpu.SMEM), qspec, prev, cur, nxt, prev, cur, nxt],
        out_specs=qspec,
        out_shape=jax.ShapeDtypeStruct(aq.shape, BF16),
        compiler_params=pltpu.CompilerParams(dimension_semantics=("arbitrary", "arbitrary"),
                                             vmem_limit_bytes=VMEM_LIMIT),
        name="swa_attention",
    )(sinks, aq, ak, ak, ak, av, av, av)


_GLA_HEAD_SLICES = [(slice(h * GLA_DK, (h + 1) * GLA_DK), slice(h * GLA_DV, (h + 1) * GLA_DV)) for h in range(GLA_HEADS)]


def _gla_decays(q_ref, k_ref, lg_ref, reverse):
    C = q_ref.shape[0]
    ri = lax.broadcasted_iota(jnp.int32, (C, C), 0)
    cj = lax.broadcasted_iota(jnp.int32, (C, C), 1)
    pair = (cj >= ri) if reverse else (cj <= ri)
    tri = pair.astype(BF16)
    lg_hi, lg_lo = _split_bf16(lg_ref[...])
    bc = _dot(tri, lg_hi) + _dot(tri, lg_lo)
    end = 0 if reverse else C - 1
    b_end = bc[end:end + 1, :]
    qt = (q_ref[...] * jnp.exp(bc)).astype(BF16)
    k_to_end = (k_ref[...] * jnp.exp(b_end - bc)).astype(BF16)
    return pair, bc, b_end, qt, k_to_end


def _gla_exact_intra(q_ref, k_ref, v_ref, bc, oi_ref, bc_ref, vf_ref, reverse):
    C = q_ref.shape[0]
    q = q_ref[...]
    bc_ref[...] = bc
    vf_ref[...] = v_ref[...].astype(F32)
    oi_ref[...] = jnp.zeros_like(oi_ref)
    rows = lax.broadcasted_iota(jnp.int32, (C, 1), 0)

    def body(j, carry):
        kj = k_ref[pl.ds(j, 1), :]
        bj = bc_ref[pl.ds(j, 1), :]
        vj = vf_ref[pl.ds(j, 1), :]
        live = (rows <= j) if reverse else (rows >= j)
        t = jnp.where(live, q * kj * jnp.exp(jnp.minimum(bc - bj, 0.0)), 0.0)
        for ks, vs in _GLA_HEAD_SLICES:
            a = jnp.sum(t[:, ks], axis=1, keepdims=True)
            oi_ref[:, vs] += a * vj[:, vs]
        return carry

    lax.fori_loop(0, C, body, 0)
    return [oi_ref[:, vs] for _, vs in _GLA_HEAD_SLICES]


def _gla_kernel(qf_ref, kf_ref, vf_ref, lgf_ref, qb_ref, kb_ref, vb_ref, lgb_ref, of_ref, ob_ref,
                stf_ref, stb_ref, oi_ref, bc_ref, vv_ref):
    @pl.when(pl.program_id(1) == 0)
    def _():
        stf_ref[...] = jnp.zeros_like(stf_ref)
        stb_ref[...] = jnp.zeros_like(stb_ref)

    dirs = [(qf_ref, kf_ref, vf_ref, lgf_ref, of_ref, stf_ref, False),
            (qb_ref, kb_ref, vb_ref, lgb_ref, ob_ref, stb_ref, True)]

    def both(fast):
        dec = [_gla_decays(q_ref, k_ref, lg_ref, rev) for q_ref, k_ref, _, lg_ref, _, _, rev in dirs]
        if fast:
            scores = []
            for (_, k_ref, _, _, _, _, _), (_, bc, _, qt, _) in zip(dirs, dec):
                kt = (k_ref[...] * jnp.exp(-bc)).astype(BF16)
                scores.append([_dot_nt(qt[:, ks], kt[:, ks]) for ks, _ in _GLA_HEAD_SLICES])
        inter, new_state = [], []
        for (_, _, v_ref, _, _, st_ref, _), (_, _, b_end, qt, k_to_end) in zip(dirs, dec):
            st = st_ref[...]
            st_b = st.astype(BF16)
            inter.append([_dot_nt(qt[:, ks], st_b[:, ks]) for ks, _ in _GLA_HEAD_SLICES])
            cols = [_dot_tn(v_ref[:, vs], k_to_end[:, ks]) for ks, vs in _GLA_HEAD_SLICES]
            new_state.append(st * jnp.exp(b_end) + jnp.concatenate(cols, axis=1))
        for d, ((q_ref, k_ref, v_ref, _, o_ref, st_ref, rev), (pair, bc, _, _, _)) in enumerate(zip(dirs, dec)):
            if fast:
                intra = [_dot(jnp.where(pair, a, 0.0).astype(BF16), v_ref[:, vs])
                         for a, (_, vs) in zip(scores[d], _GLA_HEAD_SLICES)]
            else:
                intra = _gla_exact_intra(q_ref, k_ref, v_ref, bc, oi_ref, bc_ref, vv_ref, rev)
            for (_, vs), o_inter, o_intra in zip(_GLA_HEAD_SLICES, inter[d], intra):
                o_ref[:, vs] = o_inter + o_intra
            st_ref[...] = new_state[d]

    chunk_decay = jnp.minimum(jnp.min(jnp.sum(lgf_ref[...], axis=0, keepdims=True)),
                              jnp.min(jnp.sum(lgb_ref[...], axis=0, keepdims=True)))
    fast_ok = chunk_decay >= -GLA_FAST_MAX_DECAY
    pl.when(fast_ok)(lambda: both(True))
    pl.when(jnp.logical_not(fast_ok))(lambda: both(False))


def _gla(gq, gk, gv, lgf, lgb, batch, seq):
    C = GLA_CHUNK
    nc = seq // C
    T = gq.shape[0]
    fwd = lambda w: pl.BlockSpec((C, w), lambda b, n: (b * nc + n, 0))
    bwd = lambda w: pl.BlockSpec((C, w), lambda b, n: (b * nc + nc - 1 - n, 0))
    return pl.pallas_call(
        _gla_kernel,
        grid=(batch, nc),
        in_specs=[fwd(GLA_K_W), fwd(GLA_K_W), fwd(GLA_V_W), fwd(GLA_K_W),
                  bwd(GLA_K_W), bwd(GLA_K_W), bwd(GLA_V_W), bwd(GLA_K_W)],
        out_specs=[fwd(GLA_V_W), bwd(GLA_V_W)],
        out_shape=[jax.ShapeDtypeStruct((T, GLA_V_W), F32)] * 2,
        scratch_shapes=[pltpu.VMEM((GLA_DV, GLA_K_W), F32), pltpu.VMEM((GLA_DV, GLA_K_W), F32),
                        pltpu.VMEM((C, GLA_V_W), F32), pltpu.VMEM((C, GLA_K_W), F32),
                        pltpu.VMEM((C, GLA_V_W), F32)],
        compiler_params=pltpu.CompilerParams(dimension_semantics=("arbitrary", "arbitrary"),
                                             vmem_limit_bytes=VMEM_LIMIT),
        name="gla_bidir",
    )(gq, gk, gv, lgf, gq, gk, gv, lgb)


def _layer_norm(y, g, b):
    yc = y - jnp.mean(y, axis=-1, keepdims=True)
    var = jnp.mean(yc * yc, axis=-1, keepdims=True)
    return yc * lax.rsqrt(var + LN_EPS) * g + b


def _post_mixer_kernel(of_ref, ob_ref, gr_ref, att_ref, x_ref, gng_ref, wo_ref, g1_ref, b1_ref,
                       wr_ref, br_ref, x1_ref, ei_ref, gw_ref):
    tm = x_ref.shape[0]
    mixed = [att_ref[...]]
    for h in range(GLA_HEADS):
        vs = slice(h * GLA_DV, (h + 1) * GLA_DV)
        o = of_ref[:, vs] + ob_ref[:, vs]
        o = o * lax.rsqrt(jnp.mean(o * o, axis=-1, keepdims=True) + LN_EPS) * gng_ref[...]
        r = gr_ref[:, vs].astype(F32)
        mixed.append((o * (r / (1.0 + jnp.exp(-r)))).astype(BF16))
    m = _dot(jnp.concatenate(mixed, axis=1), wo_ref[...])
    x1 = _layer_norm(DEEPNORM_ALPHA * x_ref[...] + m, g1_ref[...], b1_ref[...])
    _store_token_tiles(x1_ref, x1)

    xh, xl = _split_bf16(x1)
    hi_lo = _dot(xh, wr_ref[...])
    logits = hi_lo[:, :N_EXPERTS] + hi_lo[:, N_EXPERTS:] + _dot(xl, wr_ref[:, :N_EXPERTS]) + br_ref[...]
    lane = lax.broadcasted_iota(jnp.int32, (tm, N_EXPERTS), 1).astype(F32)
    out_lane = lax.broadcasted_iota(jnp.int32, (tm, LANES), 1)
    vals = logits
    top_v = []
    e_out = jnp.zeros((tm, LANES), jnp.int32)
    for kk in range(TOP_K):
        mx = jnp.max(vals, axis=1, keepdims=True)
        idx = jnp.min(jnp.where(vals == mx, lane, float(N_EXPERTS)), axis=1, keepdims=True)
        top_v.append(mx)
        e_out = jnp.where(out_lane == kk, idx.astype(jnp.int32), e_out)
        vals = jnp.where(lane == idx, -jnp.inf, vals)
    ex = [jnp.exp(v - top_v[0]) for v in top_v]
    den = ex[0] + ex[1] + ex[2] + ex[3]
    g_out = jnp.zeros((tm, LANES), F32)
    for kk in range(TOP_K):
        g_out = jnp.where(out_lane == kk, ex[kk] / den, g_out)
    ei_ref[...] = e_out
    gw_ref[...] = g_out


def _post_mixer(o_f, o_b, gr, att, x2, gng, w_out, g1, b1, wr_cat, br):
    T = x2.shape[0]
    tm = PROJ_TILE
    row = lambda w: pl.BlockSpec((tm, w), lambda i: (i, 0))
    full = lambda a: pl.BlockSpec(a.shape, lambda i: (0,) * a.ndim)
    return pl.pallas_call(
        _post_mixer_kernel,
        grid=(T // tm,),
        in_specs=[row(GLA_V_W), row(GLA_V_W), row(GLA_V_W), row(ATT_Q_W), row(D_MODEL), full(gng), full(w_out),
                  full(g1), full(b1), full(wr_cat), full(br)],
        out_specs=[_token_tile_spec(tm), row(LANES), row(LANES)],
        out_shape=[jax.ShapeDtypeStruct((T * TOKEN_TILE_ROWS, LANES), F32), jax.ShapeDtypeStruct((T, LANES), jnp.int32),
                   jax.ShapeDtypeStruct((T, LANES), F32)],
        compiler_params=pltpu.CompilerParams(dimension_semantics=("arbitrary",), vmem_limit_bytes=VMEM_LIMIT),
        name="post_mixer_router",
    )(o_f, o_b, gr, att, x2, gng, w_out, g1, b1, wr_cat, br)


MOE_SLOTS = 3
MOE_FF_CHUNK = 256
MOE_WINDOW = 4096
MOE_ZERO_ROWS = 1024
MOE_ACC_BATCH = 16
FLAG_RUN, FLAG_FLUSH, FLAG_CLEAR, FLAG_DRAIN = 1, 2, 4, 8


def _moe_kernel(be_ref, bw_ref, flags_ref, gcur_ref, gnext_ref, gnext2_ref, ltok_ref, gw_ref, x_hbm,
                wgu_ref, bgu_ref, wd_ref, bd_ref, f_hbm, *scratch):
    del be_ref
    R = MOE_BLOCK
    TR = TOKEN_TILE_ROWS
    i = pl.program_id(0)
    xbuf, ybuf = scratch[:MOE_SLOTS], scratch[MOE_SLOTS:2 * MOE_SLOTS]
    acc, gsem, fsem = scratch[2 * MOE_SLOTS:]
    flags = flags_ref[i]
    win_rows = MOE_WINDOW * TR

    def tile(ref, row):
        return ref.at[pl.ds(pl.multiple_of(row * TR, TR), TR), :]

    def gather_row(idx_ref, r, s):
        return pltpu.make_async_copy(tile(x_hbm, idx_ref[0, 0, r]), tile(xbuf[s], r), gsem.at[s])

    def wait_gather(s):
        pltpu.make_async_copy(x_hbm.at[pl.ds(0, R * TR), :], xbuf[s], gsem.at[s]).wait()

    def clear_acc():
        def clear(j, carry):
            acc[pl.ds(pl.multiple_of(j * MOE_ZERO_ROWS, MOE_ZERO_ROWS), MOE_ZERO_ROWS), :] = jnp.zeros(
                (MOE_ZERO_ROWS, LANES), F32)
            return carry

        lax.fori_loop(0, win_rows // MOE_ZERO_ROWS, clear, 0)

    @pl.when(i == 0)
    def _():
        def issue(r, carry):
            gather_row(gcur_ref, r, 0).start()
            gather_row(gnext_ref, r, 1).start()
            return carry

        lax.fori_loop(0, R, issue, 0)
        ybuf[MOE_SLOTS - 1][...] = jnp.zeros((R * TR, LANES), F32)
        acc[pl.ds(win_rows, TR), :] = jnp.zeros((TR, LANES), F32)

    def step(slot):
        wait_gather(slot)
        for r in range(R):
            gather_row(gnext2_ref, r, (slot + 2) % MOE_SLOTS).start()
        xb = _load_token_tiles(xbuf[slot], R).astype(BF16)
        acts = []
        for c in range(D_FF // MOE_FF_CHUNK):
            gs = slice(c * MOE_FF_CHUNK, (c + 1) * MOE_FF_CHUNK)
            us = slice(D_FF + c * MOE_FF_CHUNK, D_FF + (c + 1) * MOE_FF_CHUNK)
            gate = jnp.minimum(_dot(xb, wgu_ref[0, :, gs]) + bgu_ref[0, :, gs], SWIGLU_LIMIT)
            up = jnp.clip(_dot(xb, wgu_ref[0, :, us]) + bgu_ref[0, :, us], -SWIGLU_LIMIT, SWIGLU_LIMIT)
            acts.append(((up + 1.0) * gate / (1.0 + jnp.exp(-SWIGLU_ALPHA * gate))).astype(BF16))
        _store_token_tiles(ybuf[slot], _dot(jnp.concatenate(acts, axis=1), wd_ref[0]) + bd_ref[0])

    def accumulate(slot):
        prev = (slot + MOE_SLOTS - 1) % MOE_SLOTS
        for r0 in range(0, R, MOE_ACC_BATCH):
            batch = range(r0, r0 + MOE_ACC_BATCH)
            dsts = [tile(acc, ltok_ref[0, 0, r]) for r in batch]
            vals = [d[...] + gw_ref[0, 0, r] * ybuf[prev][r * TR:(r + 1) * TR, :] for d, r in zip(dsts, batch)]
            for d, v in zip(dsts, vals):
                d[...] = v

    for slot in range(MOE_SLOTS):
        pl.when(((flags & FLAG_RUN) != 0) & (i % MOE_SLOTS == slot))(functools.partial(accumulate, slot))
    for slot in range(MOE_SLOTS):
        pl.when(((flags & FLAG_RUN) != 0) & (i % MOE_SLOTS == slot))(functools.partial(step, slot))

    @pl.when((flags & FLAG_FLUSH) != 0)
    def _():
        start = pl.multiple_of(bw_ref[i - 1] * win_rows, win_rows)
        flush = pltpu.make_async_copy(acc.at[pl.ds(0, win_rows), :], f_hbm.at[pl.ds(start, win_rows), :], fsem.at[0])
        flush.start()
        flush.wait()

    pl.when((flags & FLAG_CLEAR) != 0)(clear_acc)

    def drain(slot):
        wait_gather((slot + 1) % MOE_SLOTS)
        wait_gather((slot + 2) % MOE_SLOTS)

    for slot in range(MOE_SLOTS):
        pl.when(((flags & FLAG_DRAIN) != 0) & (i % MOE_SLOTS == slot))(functools.partial(drain, slot))


def _moe(x1t, tables, w_gu, b_gu, w_dn, b_dn):
    R = MOE_BLOCK
    TR = TOKEN_TILE_ROWS
    block_e, block_win, flags, gtok, ltok, gwr = tables
    n_blocks = block_e.shape[0]
    smem_blk = lambda off: pl.BlockSpec((1, 1, R), lambda i, be, bw, fl: (i + off, 0, 0), memory_space=pltpu.SMEM)
    by_expert = lambda shape: pl.BlockSpec(shape, lambda i, be, bw, fl: (be[i], 0, 0))
    grid_spec = pltpu.PrefetchScalarGridSpec(
        num_scalar_prefetch=3,
        grid=(n_blocks,),
        in_specs=[
            smem_blk(0), smem_blk(1), smem_blk(2), smem_blk(0), smem_blk(0),
            pl.BlockSpec(memory_space=pl.ANY),
            by_expert((1, D_MODEL, 2 * D_FF)), by_expert((1, 1, 2 * D_FF)),
            by_expert((1, D_FF, D_MODEL)), by_expert((1, 1, D_MODEL)),
        ],
        out_specs=pl.BlockSpec(memory_space=pl.ANY),
        scratch_shapes=[pltpu.VMEM((R * TR, LANES), F32)] * (2 * MOE_SLOTS)
        + [pltpu.VMEM(((MOE_WINDOW + 1) * TR, LANES), F32),
           pltpu.SemaphoreType.DMA((MOE_SLOTS,)), pltpu.SemaphoreType.DMA((1,))],
    )
    return pl.pallas_call(
        _moe_kernel,
        grid_spec=grid_spec,
        out_shape=jax.ShapeDtypeStruct(x1t.shape, F32),
        compiler_params=pltpu.CompilerParams(dimension_semantics=("arbitrary",), vmem_limit_bytes=VMEM_LIMIT),
        name="moe_experts",
    )(block_e, block_win, flags, gtok, gtok, gtok, ltok, gwr, x1t, w_gu, b_gu, w_dn, b_dn)


def _combine_kernel(f_ref, x1_ref, g2_ref, b2_ref, out_ref):
    tm = out_ref.shape[0]
    y = DEEPNORM_ALPHA * _load_token_tiles(x1_ref, tm) + _load_token_tiles(f_ref, tm)
    out_ref[...] = _layer_norm(y, g2_ref[...], b2_ref[...])


def _combine(f, x1t, g2, b2):
    T = x1t.shape[0] // TOKEN_TILE_ROWS
    tm = COMBINE_TILE
    full = lambda a: pl.BlockSpec(a.shape, lambda i: (0,) * a.ndim)
    return pl.pallas_call(
        _combine_kernel,
        grid=(T // tm,),
        in_specs=[_token_tile_spec(tm), _token_tile_spec(tm), full(g2), full(b2)],
        out_specs=pl.BlockSpec((tm, D_MODEL), lambda i: (i, 0)),
        out_shape=jax.ShapeDtypeStruct((T, D_MODEL), F32),
        compiler_params=pltpu.CompilerParams(dimension_semantics=("arbitrary",), vmem_limit_bytes=VMEM_LIMIT),
        name="moe_combine_ln",
    )(f, x1t, g2, b2)


def _routing_tables(e_idx, gates):
    T = e_idx.shape[0]
    R, E, W = MOE_BLOCK, N_EXPERTS, MOE_WINDOW
    M = T * TOP_K
    n_seg = (T // W) * E
    m_bits = (M - 1).bit_length()
    assert (n_seg << m_bits) < 2 ** 31
    m_ids = jnp.arange(M, dtype=jnp.int32)
    seg_of_m = (m_ids // (TOP_K * W)) * E + e_idx.reshape(M)
    m_sorted = jnp.sort((seg_of_m << m_bits) + m_ids) & ((1 << m_bits) - 1)
    counts = jnp.sum((seg_of_m[:, None] == jnp.arange(n_seg, dtype=jnp.int32)[None, :]).astype(jnp.int32), axis=0)
    padded = ((counts + R - 1) // R) * R
    ends_pad = jnp.cumsum(padded)
    starts_pad = ends_pad - padded
    starts = jnp.cumsum(counts) - counts
    n_blocks = M // R + n_seg
    n_valid = ends_pad[-1] // R
    blk = jnp.arange(n_blocks, dtype=jnp.int32)
    seg = jnp.minimum(jnp.sum((blk * R)[:, None] >= ends_pad[None, :], axis=1), n_seg - 1).astype(jnp.int32)
    block_e, block_win = seg % E, seg // E
    win_first = starts_pad[block_win * E] // R
    win_last = ends_pad[block_win * E + E - 1] // R - 1
    valid_blk = blk < n_valid
    closes_window = valid_blk & (blk == win_last)
    flags = (jnp.where(blk <= n_valid, FLAG_RUN, 0)
             + jnp.where(jnp.concatenate([jnp.zeros((1,), bool), closes_window[:-1]]), FLAG_FLUSH, 0)
             + jnp.where(valid_blk & (blk == win_first), FLAG_CLEAR, 0)
             + jnp.where(blk == n_valid, FLAG_DRAIN, 0)).astype(jnp.int32)
    in_row = jnp.arange(R, dtype=jnp.int32)[None, :]
    local = (blk * R - starts_pad[seg])[:, None] + in_row
    valid = (local < counts[seg][:, None]) & valid_blk[:, None]
    m_row = m_sorted[jnp.minimum(starts[seg][:, None] + local, M - 1)]
    tok = m_row // TOP_K
    gtok = jnp.where(valid, tok, 0)
    ltok = jnp.where(valid, tok - block_win[:, None] * W, W)
    gwr = jnp.where(valid, gates.reshape(M)[m_row], 0.0)
    gtok = jnp.concatenate([gtok, jnp.zeros((2, R), jnp.int32)], axis=0)
    ltok = jnp.concatenate([jnp.full((1, R), W, jnp.int32), ltok[:-1]], axis=0)
    gwr = jnp.concatenate([jnp.zeros((1, R), F32), gwr[:-1]], axis=0)
    shape3 = lambda a: a.reshape(a.shape[0], 1, R)
    return block_e, block_win, flags, shape3(gtok), shape3(ltok), shape3(gwr)


def _rope_tables(seq):
    half = ATT_HEAD_DIM // 2
    inv_freq = ROPE_THETA ** (-jnp.arange(half, dtype=F32) * 2.0 / ATT_HEAD_DIM)
    ang = jnp.arange(seq, dtype=F32)[:, None] * inv_freq[None, :]
    cos, sin = jnp.cos(ang), jnp.sin(ang)
    reps = LANES // ATT_HEAD_DIM
    cos_t = jnp.tile(jnp.concatenate([cos, cos], axis=1), (1, reps))
    sin_t = jnp.tile(jnp.concatenate([-sin, sin], axis=1), (1, reps))
    return cos_t, sin_t


def _run_trunk(x, p):
    B, S, D = x.shape
    T = B * S
    assert T % MOE_WINDOW == 0 and T % COMBINE_TILE == 0
    x2 = x.reshape(T, D)
    cos_t, sin_t = _rope_tables(S)
    aq, ak, av, gq, gk, gv, gr, lgf, lgb = _in_proj(x2, p["w_in"], p["wgf"], p["bgf"], p["wgb"], p["bgb"],
                                                    cos_t, sin_t, S)
    att = _attention(aq, ak, av, p["sinks"], B, S)
    o_f, o_b = _gla(gq, gk, gv, lgf, lgb, B, S)
    x1t, e_pad, g_pad = _post_mixer(o_f, o_b, gr, att, x2, p["gng"], p["w_out"], p["g1"], p["b1"],
                                    p["wr_cat"], p["br"])
    tables = _routing_tables(e_pad[:, :TOP_K], g_pad[:, :TOP_K])
    f = _moe(x1t, tables, p["w_gu"], p["b_gu"], p["w_dn"], p["b_dn"])
    y = _combine(f, x1t, p["g2"], p["b2"])
    return y.reshape(B, S, D)


def kernel(x_prompt, x_sample, w_in, w_gate_f, b_gate_f, w_gate_b, b_gate_b, sinks, gla_norm_g, w_out, ln1_g, ln1_b,
           w_router, b_router, w_gate_up, b_gate_up, w_down, b_down, ln2_g, ln2_b):
    l = 0
    wr = w_router[l]
    wr_hi = wr.astype(BF16)
    p = dict(
        w_in=w_in[l].astype(BF16),
        wgf=w_gate_f[l].astype(BF16), bgf=b_gate_f[l].reshape(1, -1),
        wgb=w_gate_b[l].astype(BF16), bgb=b_gate_b[l].reshape(1, -1),
        sinks=sinks[l], gng=gla_norm_g[l].reshape(1, -1), w_out=w_out[l].astype(BF16),
        g1=ln1_g[l].reshape(1, -1), b1=ln1_b[l].reshape(1, -1),
        wr_cat=jnp.concatenate([wr_hi, (wr - wr_hi.astype(F32)).astype(BF16)], axis=1), br=b_router[l].reshape(1, -1),
        w_gu=w_gate_up[l].astype(BF16), b_gu=b_gate_up[l].reshape(N_EXPERTS, 1, -1),
        w_dn=w_down[l].astype(BF16), b_dn=b_down[l].reshape(N_EXPERTS, 1, -1),
        g2=ln2_g[l].reshape(1, -1), b2=ln2_b[l].reshape(1, -1),
    )
    return (_run_trunk(x_prompt, p), _run_trunk(x_sample, p))
```

```python
import functools

import jax
import jax.numpy as jnp
import numpy as np
from jax import lax
from jax.experimental import pallas as pl
from jax.experimental.pallas import tpu as pltpu

D_MODEL = 1024
DEPTH = 1
ATT_HEADS = 8
ATT_KV_HEADS = 2
ATT_HEAD_DIM = 64
ATT_GROUP = ATT_HEADS // ATT_KV_HEADS
WINDOW = 128
ATT_BLOCK = 128
ROPE_THETA = 10000.0
GLA_HEADS = 4
GLA_DK = 64
GLA_DV = 128
GLA_GATE_RANK = 16
GLA_TAU = 16.0
N_EXPERTS = 32
TOP_K = 4
D_FF = 1024
SWIGLU_LIMIT = 7.0
SWIGLU_ALPHA = 1.702
MOE_BLOCK = 256
LN_EPS = 1e-5
DEEPNORM_ALPHA = (2 * DEPTH) ** 0.25

ATT_Q_W = ATT_HEADS * ATT_HEAD_DIM
ATT_KV_W = ATT_KV_HEADS * ATT_HEAD_DIM
GLA_K_W = GLA_HEADS * GLA_DK
GLA_V_W = GLA_HEADS * GLA_DV
OFF_AQ = 0
OFF_AK = OFF_AQ + ATT_Q_W
OFF_AV = OFF_AK + ATT_KV_W
OFF_GQ = OFF_AV + ATT_KV_W
OFF_GK = OFF_GQ + GLA_K_W
OFF_GV = OFF_GK + GLA_K_W
OFF_GR = OFF_GV + GLA_V_W
OFF_Z = OFF_GR + GLA_V_W
IN_WIDTH = OFF_Z + 2 * GLA_GATE_RANK

LANES = 128
PROJ_TILE = 512
GLA_CHUNK = 128
GLA_FAST_MAX_DECAY = 60.0
COMBINE_TILE = 512
NEG_BIG = -1e30
VMEM_LIMIT = 48 * 1024 * 1024

BF16 = jnp.bfloat16
F32 = jnp.float32


def _dot(a, b):
    return jnp.dot(a, b, preferred_element_type=F32)


def _dot_nt(a, b):
    return lax.dot_general(a, b, (((1,), (1,)), ((), ())), preferred_element_type=F32)


def _dot_tn(a, b):
    return lax.dot_general(a, b, (((0,), (0,)), ((), ())), preferred_element_type=F32)


def _split_bf16(x):
    hi = x.astype(BF16)
    lo = (x - hi.astype(F32)).astype(BF16)
    return hi, lo


TOKEN_TILE_ROWS = D_MODEL // LANES


def _token_tile_spec(tokens, index_map=lambda i: (i, 0)):
    return pl.BlockSpec((tokens * TOKEN_TILE_ROWS, LANES), index_map)


def _load_token_tiles(ref, tokens):
    return jnp.concatenate([ref[pl.ds(c, tokens, stride=TOKEN_TILE_ROWS), :] for c in range(TOKEN_TILE_ROWS)],
                           axis=1)


def _store_token_tiles(ref, x):
    for c in range(TOKEN_TILE_ROWS):
        ref[pl.ds(c, x.shape[0], stride=TOKEN_TILE_ROWS), :] = x[:, c * LANES:(c + 1) * LANES]


def _in_proj_kernel(x_ref, w_ref, wgf_ref, bgf_ref, wgb_ref, bgb_ref, cos_ref, sin_ref,
                    aq_ref, ak_ref, av_ref, gq_ref, gk_ref, gv_ref, gr_ref, lgf_ref, lgb_ref):
    xb = x_ref[...].astype(BF16)
    cos = cos_ref[...]
    sin = sin_ref[...]
    lane = lax.broadcasted_iota(jnp.int32, (x_ref.shape[0], LANES), 1)
    first_half = (lane % ATT_HEAD_DIM) < (ATT_HEAD_DIM // 2)

    def rope(t):
        swapped = jnp.where(first_half, pltpu.roll(t, LANES - ATT_HEAD_DIM // 2, 1),
                            pltpu.roll(t, ATT_HEAD_DIM // 2, 1))
        return t * cos + swapped * sin

    def proj(lo, width):
        return _dot(xb, w_ref[:, lo:lo + width])

    q_scale = ATT_HEAD_DIM ** -0.5
    t = proj(OFF_AQ, ATT_Q_W)
    for c in range(ATT_Q_W // LANES):
        cs = slice(c * LANES, (c + 1) * LANES)
        aq_ref[:, cs] = (rope(t[:, cs]) * q_scale).astype(aq_ref.dtype)
    kv = proj(OFF_AK, 2 * ATT_KV_W)
    ak_ref[...] = rope(kv[:, :ATT_KV_W]).astype(ak_ref.dtype)
    av_ref[...] = kv[:, ATT_KV_W:].astype(av_ref.dtype)
    gq_ref[...] = proj(OFF_GQ, GLA_K_W) * (GLA_DK ** -0.5)
    gk_ref[...] = proj(OFF_GK, GLA_K_W)
    gv_ref[...] = proj(OFF_GV, GLA_V_W).astype(gv_ref.dtype)
    grz = proj(OFF_GR, GLA_V_W + 2 * GLA_GATE_RANK)
    gr_ref[...] = grz[:, :GLA_V_W].astype(gr_ref.dtype)
    z = grz[:, GLA_V_W:]

    def log_decay(zr, wg_ref, bg_ref):
        pre = _dot(zr.astype(BF16), wg_ref[...]) + bg_ref[...]
        log_sig = jnp.minimum(pre, 0.0) - jnp.log(1.0 + jnp.exp(-jnp.abs(pre)))
        return log_sig * (1.0 / GLA_TAU)

    lgf_ref[...] = log_decay(z[:, :GLA_GATE_RANK], wgf_ref, bgf_ref)
    lgb_ref[...] = log_decay(z[:, GLA_GATE_RANK:], wgb_ref, bgb_ref)


def _in_proj(x2, w_in, wgf, bgf, wgb, bgb, cos_t, sin_t, seq):
    T = x2.shape[0]
    tm = PROJ_TILE
    assert T % tm == 0 and seq % tm == 0
    pos_tiles = seq // tm
    row = lambda w: pl.BlockSpec((tm, w), lambda i: (i, 0))
    full = lambda a: pl.BlockSpec(a.shape, lambda i: (0,) * a.ndim)
    pos = pl.BlockSpec((tm, LANES), lambda i: (i % pos_tiles, 0))
    out_w = [(ATT_Q_W, BF16), (ATT_KV_W, BF16), (ATT_KV_W, BF16), (GLA_K_W, F32), (GLA_K_W, F32),
             (GLA_V_W, BF16), (GLA_V_W, BF16), (GLA_K_W, F32), (GLA_K_W, F32)]
    return pl.pallas_call(
        _in_proj_kernel,
        grid=(T // tm,),
        in_specs=[row(D_MODEL), full(w_in), full(wgf), full(bgf), full(wgb), full(bgb), pos, pos],
        out_specs=[row(w) for w, _ in out_w],
        out_shape=[jax.ShapeDtypeStruct((T, w), dt) for w, dt in out_w],
        compiler_params=pltpu.CompilerParams(dimension_semantics=("arbitrary",), vmem_limit_bytes=VMEM_LIMIT),
        name="in_proj",
    )(x2, w_in, wgf, bgf, wgb, bgb, cos_t, sin_t)


def _attention_kernel(sinks_ref, q_ref, kp_ref, kc_ref, kn_ref, vp_ref, vc_ref, vn_ref, o_ref, *, seq):
    n = pl.program_id(1)
    R = ATT_BLOCK
    D = ATT_HEAD_DIM
    kcat = jnp.concatenate([kp_ref[...], kc_ref[...], kn_ref[...]], axis=0)
    vcat = jnp.concatenate([vp_ref[...], vc_ref[...], vn_ref[...]], axis=0)
    qi = lax.broadcasted_iota(jnp.int32, (R, 3 * R), 0)
    kj = lax.broadcasted_iota(jnp.int32, (R, 3 * R), 1)
    rel = kj - R - qi
    key_pos = n * R - R + kj
    valid = (jnp.abs(rel) <= WINDOW) & (key_pos >= 0) & (key_pos < seq)
    assert ATT_KV_HEADS * D == LANES
    v_lane = lax.broadcasted_iota(jnp.int32, vcat.shape, 1)
    v_aug = [jnp.where((v_lane < D) == (kv == 0), vcat, jnp.ones_like(vcat)) for kv in range(ATT_KV_HEADS)]

    scores = []
    for h in range(ATT_HEADS):
        kv = h // ATT_GROUP
        s = _dot_nt(q_ref[:, h * D:(h + 1) * D], kcat[:, kv * D:(kv + 1) * D])
        scores.append(jnp.where(valid, s, NEG_BIG))
    probs, sink_terms = [], []
    for h in range(ATT_HEADS):
        sink = sinks_ref[h]
        m = jnp.maximum(jnp.max(scores[h], axis=1, keepdims=True), sink)
        probs.append(jnp.exp(scores[h] - m).astype(BF16))
        sink_terms.append(jnp.exp(sink - m))
    for h in range(ATT_HEADS):
        kv = h // ATT_GROUP
        res = _dot(probs[h], v_aug[kv])
        val = res[:, kv * D:(kv + 1) * D]
        den = res[:, (1 - kv) * D:(1 - kv) * D + 1] + sink_terms[h]
        o_ref[:, h * D:(h + 1) * D] = (val / den).astype(o_ref.dtype)


def _attention(aq, ak, av, sinks, batch, seq):
    R = ATT_BLOCK
    nb = seq // R
    qspec = pl.BlockSpec((R, ATT_Q_W), lambda b, n: (b * nb + n, 0))
    prev = pl.BlockSpec((R, ATT_KV_W), lambda b, n: (b * nb + jnp.maximum(n - 1, 0), 0))
    cur = pl.BlockSpec((R, ATT_KV_W), lambda b, n: (b * nb + n, 0))
    nxt = pl.BlockSpec((R, ATT_KV_W), lambda b, n: (b * nb + jnp.minimum(n + 1, nb - 1), 0))
    return pl.pallas_call(
        functools.partial(_attention_kernel, seq=seq),
        grid=(batch, nb),
        in_specs=[pl.BlockSpec(memory_space=pltpu.SMEM), qspec, prev, cur, nxt, prev, cur, nxt],
        out_specs=qspec,
        out_shape=jax.ShapeDtypeStruct(aq.shape, BF16),
        compiler_params=pltpu.CompilerParams(dimension_semantics=("arbitrary", "arbitrary"),
                                             vmem_limit_bytes=VMEM_LIMIT),
        name="swa_attention",
    )(sinks, aq, ak, ak, ak, av, av, av)


_GLA_HEAD_SLICES = [(slice(h * GLA_DK, (h + 1) * GLA_DK), slice(h * GLA_DV, (h + 1) * GLA_DV)) for h in range(GLA_HEADS)]


def _gla_decays(q_ref, k_ref, lg_ref, reverse):
    C = q_ref.shape[0]
    ri = lax.broadcasted_iota(jnp.int32, (C, C), 0)
    cj = lax.broadcasted_iota(jnp.int32, (C, C), 1)
    pair = (cj >= ri) if reverse else (cj <= ri)
    tri = pair.astype(BF16)
    lg_hi, lg_lo = _split_bf16(lg_ref[...])
    bc = _dot(tri, lg_hi) + _dot(tri, lg_lo)
    end = 0 if reverse else C - 1
    b_end = bc[end:end + 1, :]
    qt = (q_ref[...] * jnp.exp(bc)).astype(BF16)
    k_to_end = (k_ref[...] * jnp.exp(b_end - bc)).astype(BF16)
    return pair, bc, b_end, qt, k_to_end


def _gla_exact_intra(q_ref, k_ref, v_ref, bc, oi_ref, bc_ref, vf_ref, reverse):
    C = q_ref.shape[0]
    q = q_ref[...]
    bc_ref[...] = bc
    vf_ref[...] = v_ref[...].astype(F32)
    oi_ref[...] = jnp.zeros_like(oi_ref)
    rows = lax.broadcasted_iota(jnp.int32, (C, 1), 0)

    def body(j, carry):
        kj = k_ref[pl.ds(j, 1), :]
        bj = bc_ref[pl.ds(j, 1), :]
        vj = vf_ref[pl.ds(j, 1), :]
        live = (rows <= j) if reverse else (rows >= j)
        t = jnp.where(live, q * kj * jnp.exp(jnp.minimum(bc - bj, 0.0)), 0.0)
        for ks, vs in _GLA_HEAD_SLICES:
            a = jnp.sum(t[:, ks], axis=1, keepdims=True)
            oi_ref[:, vs] += a * vj[:, vs]
        return carry

    lax.fori_loop(0, C, body, 0)
    return [oi_ref[:, vs] for _, vs in _GLA_HEAD_SLICES]


def _gla_kernel(qf_ref, kf_ref, vf_ref, lgf_ref, qb_ref, kb_ref, vb_ref, lgb_ref, of_ref, ob_ref,
                stf_ref, stb_ref, oi_ref, bc_ref, vv_ref):
    @pl.when(pl.program_id(1) == 0)
    def _():
        stf_ref[...] = jnp.zeros_like(stf_ref)
        stb_ref[...] = jnp.zeros_like(stb_ref)

    dirs = [(qf_ref, kf_ref, vf_ref, lgf_ref, of_ref, stf_ref, False),
            (qb_ref, kb_ref, vb_ref, lgb_ref, ob_ref, stb_ref, True)]

    def both(fast):
        dec = [_gla_decays(q_ref, k_ref, lg_ref, rev) for q_ref, k_ref, _, lg_ref, _, _, rev in dirs]
        if fast:
            scores = []
            for (_, k_ref, _, _, _, _, _), (_, bc, _, qt, _) in zip(dirs, dec):
                kt = (k_ref[...] * jnp.exp(-bc)).astype(BF16)
                scores.append([_dot_nt(qt[:, ks], kt[:, ks]) for ks, _ in _GLA_HEAD_SLICES])
        inter, new_state = [], []
        for (_, _, v_ref, _, _, st_ref, _), (_, _, b_end, qt, k_to_end) in zip(dirs, dec):
            st = st_ref[...]
            st_b = st.astype(BF16)
            inter.append([_dot_nt(qt[:, ks], st_b[:, ks]) for ks, _ in _GLA_HEAD_SLICES])
            cols = [_dot_tn(v_ref[:, vs], k_to_end[:, ks]) for ks, vs in _GLA_HEAD_SLICES]
            new_state.append(st * jnp.exp(b_end) + jnp.concatenate(cols, axis=1))
        for d, ((q_ref, k_ref, v_ref, _, o_ref, st_ref, rev), (pair, bc, _, _, _)) in enumerate(zip(dirs, dec)):
            if fast:
                intra = [_dot(jnp.where(pair, a, 0.0).astype(BF16), v_ref[:, vs])
                         for a, (_, vs) in zip(scores[d], _GLA_HEAD_SLICES)]
            else:
                intra = _gla_exact_intra(q_ref, k_ref, v_ref, bc, oi_ref, bc_ref, vv_ref, rev)
            for (_, vs), o_inter, o_intra in zip(_GLA_HEAD_SLICES, inter[d], intra):
                o_ref[:, vs] = o_inter + o_intra
            st_ref[...] = new_state[d]

    chunk_decay = jnp.minimum(jnp.min(jnp.sum(lgf_ref[...], axis=0, keepdims=True)),
                              jnp.min(jnp.sum(lgb_ref[...], axis=0, keepdims=True)))
    fast_ok = chunk_decay >= -GLA_FAST_MAX_DECAY
    pl.when(fast_ok)(lambda: both(True))
    pl.when(jnp.logical_not(fast_ok))(lambda: both(False))


def _gla(gq, gk, gv, lgf, lgb, batch, seq):
    C = GLA_CHUNK
    nc = seq // C
    T = gq.shape[0]
    fwd = lambda w: pl.BlockSpec((C, w), lambda b, n: (b * nc + n, 0))
    bwd = lambda w: pl.BlockSpec((C, w), lambda b, n: (b * nc + nc - 1 - n, 0))
    return pl.pallas_call(
        _gla_kernel,
        grid=(batch, nc),
        in_specs=[fwd(GLA_K_W), fwd(GLA_K_W), fwd(GLA_V_W), fwd(GLA_K_W),
                  bwd(GLA_K_W), bwd(GLA_K_W), bwd(GLA_V_W), bwd(GLA_K_W)],
        out_specs=[fwd(GLA_V_W), bwd(GLA_V_W)],
        out_shape=[jax.ShapeDtypeStruct((T, GLA_V_W), F32)] * 2,
        scratch_shapes=[pltpu.VMEM((GLA_DV, GLA_K_W), F32), pltpu.VMEM((GLA_DV, GLA_K_W), F32),
                        pltpu.VMEM((C, GLA_V_W), F32), pltpu.VMEM((C, GLA_K_W), F32),
                        pltpu.VMEM((C, GLA_V_W), F32)],
        compiler_params=pltpu.CompilerParams(dimension_semantics=("arbitrary", "arbitrary"),
                                             vmem_limit_bytes=VMEM_LIMIT),
        name="gla_bidir",
    )(gq, gk, gv, lgf, gq, gk, gv, lgb)


def _layer_norm(y, g, b):
    yc = y - jnp.mean(y, axis=-1, keepdims=True)
    var = jnp.mean(yc * yc, axis=-1, keepdims=True)
    return yc * lax.rsqrt(var + LN_EPS) * g + b


def _post_mixer_kernel(of_ref, ob_ref, gr_ref, att_ref, x_ref, gng_ref, wo_ref, g1_ref, b1_ref,
                       wr_ref, br_ref, x1_ref, ei_ref, gw_ref):
    tm = x_ref.shape[0]
    mixed = [att_ref[...]]
    for h in range(GLA_HEADS):
        vs = slice(h * GLA_DV, (h + 1) * GLA_DV)
        o = of_ref[:, vs] + ob_ref[:, vs]
        o = o * lax.rsqrt(jnp.mean(o * o, axis=-1, keepdims=True) + LN_EPS) * gng_ref[...]
        r = gr_ref[:, vs].astype(F32)
        mixed.append((o * (r / (1.0 + jnp.exp(-r)))).astype(BF16))
    m = _dot(jnp.concatenate(mixed, axis=1), wo_ref[...])
    x1 = _layer_norm(DEEPNORM_ALPHA * x_ref[...] + m, g1_ref[...], b1_ref[...])
    _store_token_tiles(x1_ref, x1)

    xh, xl = _split_bf16(x1)
    hi_lo = _dot(xh, wr_ref[...])
    logits = hi_lo[:, :N_EXPERTS] + hi_lo[:, N_EXPERTS:] + _dot(xl, wr_ref[:, :N_EXPERTS]) + br_ref[...]
    lane = lax.broadcasted_iota(jnp.int32, (tm, N_EXPERTS), 1).astype(F32)
    out_lane = lax.broadcasted_iota(jnp.int32, (tm, LANES), 1)
    vals = logits
    top_v = []
    e_out = jnp.zeros((tm, LANES), jnp.int32)
    for kk in range(TOP_K):
        mx = jnp.max(vals, axis=1, keepdims=True)
        idx = jnp.min(jnp.where(vals == mx, lane, float(N_EXPERTS)), axis=1, keepdims=True)
        top_v.append(mx)
        e_out = jnp.where(out_lane == kk, idx.astype(jnp.int32), e_out)
        vals = jnp.where(lane == idx, -jnp.inf, vals)
    ex = [jnp.exp(v - top_v[0]) for v in top_v]
    den = ex[0] + ex[1] + ex[2] + ex[3]
    g_out = jnp.zeros((tm, LANES), F32)
    for kk in range(TOP_K):
        g_out = jnp.where(out_lane == kk, ex[kk] / den, g_out)
    ei_ref[...] = e_out
    gw_ref[...] = g_out


def _post_mixer(o_f, o_b, gr, att, x2, gng, w_out, g1, b1, wr_cat, br):
    T = x2.shape[0]
    tm = PROJ_TILE
    row = lambda w: pl.BlockSpec((tm, w), lambda i: (i, 0))
    full = lambda a: pl.BlockSpec(a.shape, lambda i: (0,) * a.ndim)
    return pl.pallas_call(
        _post_mixer_kernel,
        grid=(T // tm,),
        in_specs=[row(GLA_V_W), row(GLA_V_W), row(GLA_V_W), row(ATT_Q_W), row(D_MODEL), full(gng), full(w_out),
                  full(g1), full(b1), full(wr_cat), full(br)],
        out_specs=[_token_tile_spec(tm), row(LANES), row(LANES)],
        out_shape=[jax.ShapeDtypeStruct((T * TOKEN_TILE_ROWS, LANES), F32), jax.ShapeDtypeStruct((T, LANES), jnp.int32),
                   jax.ShapeDtypeStruct((T, LANES), F32)],
        compiler_params=pltpu.CompilerParams(dimension_semantics=("arbitrary",), vmem_limit_bytes=VMEM_LIMIT),
        name="post_mixer_router",
    )(o_f, o_b, gr, att, x2, gng, w_out, g1, b1, wr_cat, br)


MOE_SLOTS = 3
MOE_FF_CHUNK = 256


def _moe_kernel(be_ref, dprev_ref, scur_ref, snext_ref, snext2_ref, x_hbm, wgu_ref, bgu_ref, wd_ref, bd_ref, y_hbm,
                *scratch, plane):
    del be_ref
    R = MOE_BLOCK
    TR = TOKEN_TILE_ROWS
    i = pl.program_id(0)
    xbuf, ybuf = scratch[:MOE_SLOTS], scratch[MOE_SLOTS:2 * MOE_SLOTS]
    gsem, ssem = scratch[2 * MOE_SLOTS:]

    def tile(ref, row):
        return ref.at[pl.ds(pl.multiple_of(row * TR, TR), TR), :]

    def gather_row(idx_ref, r, s):
        tok = idx_ref[0, 0, r] & (plane - 1)
        return pltpu.make_async_copy(tile(x_hbm, tok), tile(xbuf[s], r), gsem.at[s])

    def scatter_row(idx_ref, r, s):
        return pltpu.make_async_copy(tile(ybuf[s], r), tile(y_hbm, idx_ref[0, 0, r]), ssem.at[s])

    def wait_gather(s):
        pltpu.make_async_copy(x_hbm.at[pl.ds(0, R * TR), :], xbuf[s], gsem.at[s]).wait()

    def wait_scatter(s):
        pltpu.make_async_copy(ybuf[s], y_hbm.at[pl.ds(0, R * TR), :], ssem.at[s]).wait()

    @pl.when(i == 0)
    def _():
        def issue(r, carry):
            gather_row(scur_ref, r, 0).start()
            gather_row(snext_ref, r, 1).start()
            return carry

        lax.fori_loop(0, R, issue, 0)
        ybuf[MOE_SLOTS - 1][...] = jnp.zeros((R * TR, LANES), F32)

    def step(slot):
        ahead = (slot + 2) % MOE_SLOTS
        wait_gather(slot)

        @pl.when(i >= MOE_SLOTS - 1)
        def _():
            wait_scatter(slot)

        for r in range(R):
            gather_row(snext2_ref, r, ahead).start()
            scatter_row(dprev_ref, r, ahead).start()

        xb = _load_token_tiles(xbuf[slot], R).astype(BF16)
        acts = []
        for c in range(D_FF // MOE_FF_CHUNK):
            gs = slice(c * MOE_FF_CHUNK, (c + 1) * MOE_FF_CHUNK)
            us = slice(D_FF + c * MOE_FF_CHUNK, D_FF + (c + 1) * MOE_FF_CHUNK)
            gate = jnp.minimum(_dot(xb, wgu_ref[0, :, gs]) + bgu_ref[0, :, gs], SWIGLU_LIMIT)
            up = jnp.clip(_dot(xb, wgu_ref[0, :, us]) + bgu_ref[0, :, us], -SWIGLU_LIMIT, SWIGLU_LIMIT)
            acts.append(((up + 1.0) * gate / (1.0 + jnp.exp(-SWIGLU_ALPHA * gate))).astype(BF16))
        act = jnp.concatenate(acts, axis=1)
        _store_token_tiles(ybuf[slot], _dot(act, wd_ref[0]) + bd_ref[0])

        @pl.when(i == pl.num_programs(0) - 1)
        def _():
            wait_gather((slot + 1) % MOE_SLOTS)
            wait_gather(ahead)
            wait_scatter((slot + 1) % MOE_SLOTS)
            wait_scatter(ahead)

    for slot in range(MOE_SLOTS):
        pl.when(i % MOE_SLOTS == slot)(functools.partial(step, slot))


def _moe(x1t, block_e, row_code, w_gu, b_gu, w_dn, b_dn, plane):
    R = MOE_BLOCK
    TR = TOKEN_TILE_ROWS
    steps = block_e.shape[0]
    smem_blk = lambda off: pl.BlockSpec((1, 1, R), lambda i, be: (i + off, 0, 0), memory_space=pltpu.SMEM)
    grid_spec = pltpu.PrefetchScalarGridSpec(
        num_scalar_prefetch=1,
        grid=(steps,),
        in_specs=[
            smem_blk(0), smem_blk(1), smem_blk(2), smem_blk(3),
            pl.BlockSpec(memory_space=pl.ANY),
            pl.BlockSpec((1, D_MODEL, 2 * D_FF), lambda i, be: (be[i], 0, 0)),
            pl.BlockSpec((1, 1, 2 * D_FF), lambda i, be: (be[i], 0, 0)),
            pl.BlockSpec((1, D_FF, D_MODEL), lambda i, be: (be[i], 0, 0)),
            pl.BlockSpec((1, 1, D_MODEL), lambda i, be: (be[i], 0, 0)),
        ],
        out_specs=pl.BlockSpec(memory_space=pl.ANY),
        scratch_shapes=[pltpu.VMEM((R * TR, LANES), F32)] * (2 * MOE_SLOTS)
        + [pltpu.SemaphoreType.DMA((MOE_SLOTS,)), pltpu.SemaphoreType.DMA((MOE_SLOTS,))],
    )
    return pl.pallas_call(
        functools.partial(_moe_kernel, plane=plane),
        grid_spec=grid_spec,
        out_shape=jax.ShapeDtypeStruct(((TOP_K * plane + 2 * R) * TR, LANES), F32),
        compiler_params=pltpu.CompilerParams(dimension_semantics=("arbitrary",), vmem_limit_bytes=VMEM_LIMIT),
        name="moe_experts",
    )(block_e, row_code, row_code, row_code, row_code, x1t, w_gu, b_gu, w_dn, b_dn)


def _combine_kernel(y0_ref, y1_ref, y2_ref, y3_ref, gw_ref, x1_ref, g2_ref, b2_ref, out_ref):
    tm = out_ref.shape[0]
    gw = gw_ref[...]
    f = gw[:, 0:1] * _load_token_tiles(y0_ref, tm)
    for kk, y_ref in enumerate((y1_ref, y2_ref, y3_ref), start=1):
        f = f + gw[:, kk:kk + 1] * _load_token_tiles(y_ref, tm)
    out_ref[...] = _layer_norm(DEEPNORM_ALPHA * _load_token_tiles(x1_ref, tm) + f, g2_ref[...], b2_ref[...])


def _combine(y4, gw, x1t, g2, b2, plane):
    T = gw.shape[0]
    tm = COMBINE_TILE
    plane_tiles = plane // tm
    row = lambda w: pl.BlockSpec((tm, w), lambda i: (i, 0))
    full = lambda a: pl.BlockSpec(a.shape, lambda i: (0,) * a.ndim)
    slot_plane = lambda kk: _token_tile_spec(tm, lambda i: (kk * plane_tiles + i, 0))
    return pl.pallas_call(
        _combine_kernel,
        grid=(T // tm,),
        in_specs=[slot_plane(0), slot_plane(1), slot_plane(2), slot_plane(3), row(LANES), _token_tile_spec(tm),
                  full(g2), full(b2)],
        out_specs=row(D_MODEL),
        out_shape=jax.ShapeDtypeStruct((T, D_MODEL), F32),
        compiler_params=pltpu.CompilerParams(dimension_semantics=("arbitrary",), vmem_limit_bytes=VMEM_LIMIT),
        name="moe_combine_ln",
    )(y4, y4, y4, y4, gw, x1t, g2, b2)


def _routing_tables(e_idx, plane):
    T = e_idx.shape[0]
    R, E = MOE_BLOCK, N_EXPERTS
    M = T * TOP_K
    m_bits = (M - 1).bit_length()
    assert (E << m_bits) < 2 ** 31
    flat_e = e_idx.reshape(M)
    m_sorted = jnp.sort((flat_e << m_bits) + jnp.arange(M, dtype=jnp.int32)) & ((1 << m_bits) - 1)
    code_sorted = (m_sorted % TOP_K) * plane + m_sorted // TOP_K
    counts = jnp.sum((flat_e[:, None] == jnp.arange(E, dtype=jnp.int32)[None, :]).astype(jnp.int32), axis=0)
    padded = ((counts + R - 1) // R) * R
    ends_pad = jnp.cumsum(padded)
    starts_pad = ends_pad - padded
    starts = jnp.cumsum(counts) - counts
    n_blocks = M // R + E
    block_start = jnp.arange(n_blocks + 1, dtype=jnp.int32) * R
    block_e = jnp.minimum(jnp.sum(block_start[:, None] >= ends_pad[None, :], axis=1), E - 1).astype(jnp.int32)
    be = block_e[:n_blocks]
    in_row = jnp.arange(R, dtype=jnp.int32)[None, :]
    local = (block_start[:n_blocks] - starts_pad[be])[:, None] + in_row
    valid = local < counts[be][:, None]
    src = jnp.minimum(starts[be][:, None] + local, M - 1)
    parity = (jnp.arange(n_blocks + 4, dtype=jnp.int32) % 2)[:, None]
    spare = TOP_K * plane + parity * R + in_row
    body = jnp.where(valid, code_sorted[src], spare[1:n_blocks + 1])
    row_code = jnp.concatenate([spare[:1], body, spare[n_blocks + 1:]], axis=0)
    return row_code.reshape(n_blocks + 4, 1, R), block_e


def _rope_tables(seq):
    half = ATT_HEAD_DIM // 2
    inv_freq = ROPE_THETA ** (-jnp.arange(half, dtype=F32) * 2.0 / ATT_HEAD_DIM)
    ang = jnp.arange(seq, dtype=F32)[:, None] * inv_freq[None, :]
    cos, sin = jnp.cos(ang), jnp.sin(ang)
    reps = LANES // ATT_HEAD_DIM
    cos_t = jnp.tile(jnp.concatenate([cos, cos], axis=1), (1, reps))
    sin_t = jnp.tile(jnp.concatenate([-sin, sin], axis=1), (1, reps))
    return cos_t, sin_t


def _run_trunk(x, p):
    B, S, D = x.shape
    T = B * S
    plane = max(1 << (T - 1).bit_length(), COMBINE_TILE)
    assert T % COMBINE_TILE == 0 and T >= 2 * MOE_BLOCK
    x2 = x.reshape(T, D)
    cos_t, sin_t = _rope_tables(S)
    aq, ak, av, gq, gk, gv, gr, lgf, lgb = _in_proj(x2, p["w_in"], p["wgf"], p["bgf"], p["wgb"], p["bgb"],
                                                    cos_t, sin_t, S)
    att = _attention(aq, ak, av, p["sinks"], B, S)
    o_f, o_b = _gla(gq, gk, gv, lgf, lgb, B, S)
    x1t, e_pad, g_pad = _post_mixer(o_f, o_b, gr, att, x2, p["gng"], p["w_out"], p["g1"], p["b1"],
                                    p["wr_cat"], p["br"])
    row_code, block_e = _routing_tables(e_pad[:, :TOP_K], plane)
    y4 = _moe(x1t, block_e, row_code, p["w_gu"], p["b_gu"], p["w_dn"], p["b_dn"], plane)
    y = _combine(y4, g_pad, x1t, p["g2"], p["b2"], plane)
    return y.reshape(B, S, D)


def kernel(x_prompt, x_sample, w_in, w_gate_f, b_gate_f, w_gate_b, b_gate_b, sinks, gla_norm_g, w_out, ln1_g, ln1_b,
           w_router, b_router, w_gate_up, b_gate_up, w_down, b_down, ln2_g, ln2_b):
    l = 0
    wr = w_router[l]
    wr_hi = wr.astype(BF16)
    p = dict(
        w_in=w_in[l].astype(BF16),
        wgf=w_gate_f[l].astype(BF16), bgf=b_gate_f[l].reshape(1, -1),
        wgb=w_gate_b[l].astype(BF16), bgb=b_gate_b[l].reshape(1, -1),
        sinks=sinks[l], gng=gla_norm_g[l].reshape(1, -1), w_out=w_out[l].astype(BF16),
        g1=ln1_g[l].reshape(1, -1), b1=ln1_b[l].reshape(1, -1),
        wr_cat=jnp.concatenate([wr_hi, (wr - wr_hi.astype(F32)).astype(BF16)], axis=1), br=b_router[l].reshape(1, -1),
        w_gu=w_gate_up[l].astype(BF16), b_gu=b_gate_up[l].reshape(N_EXPERTS, 1, -1),
        w_dn=w_down[l].astype(BF16), b_dn=b_down[l].reshape(N_EXPERTS, 1, -1),
        g2=ln2_g[l].reshape(1, -1), b2=ln2_b[l].reshape(1, -1),
    )
    return (_run_trunk(x_prompt, p), _run_trunk(x_sample, p))
```

```python
import functools

import jax
import jax.numpy as jnp
import numpy as np
from jax import lax
from jax.experimental import pallas as pl
from jax.experimental.pallas import tpu as pltpu

D_MODEL = 1024
DEPTH = 1
ATT_HEADS = 8
ATT_KV_HEADS = 2
ATT_HEAD_DIM = 64
ATT_GROUP = ATT_HEADS // ATT_KV_HEADS
WINDOW = 128
ATT_BLOCK = 128
ROPE_THETA = 10000.0
GLA_HEADS = 4
GLA_DK = 64
GLA_DV = 128
GLA_GATE_RANK = 16
GLA_TAU = 16.0
N_EXPERTS = 32
TOP_K = 4
D_FF = 1024
SWIGLU_LIMIT = 7.0
SWIGLU_ALPHA = 1.702
MOE_BLOCK = 256
LN_EPS = 1e-5
DEEPNORM_ALPHA = (2 * DEPTH) ** 0.25

ATT_Q_W = ATT_HEADS * ATT_HEAD_DIM
ATT_KV_W = ATT_KV_HEADS * ATT_HEAD_DIM
GLA_K_W = GLA_HEADS * GLA_DK
GLA_V_W = GLA_HEADS * GLA_DV
OFF_AQ = 0
OFF_AK = OFF_AQ + ATT_Q_W
OFF_AV = OFF_AK + ATT_KV_W
OFF_GQ = OFF_AV + ATT_KV_W
OFF_GK = OFF_GQ + GLA_K_W
OFF_GV = OFF_GK + GLA_K_W
OFF_GR = OFF_GV + GLA_V_W
OFF_Z = OFF_GR + GLA_V_W
IN_WIDTH = OFF_Z + 2 * GLA_GATE_RANK

LANES = 128
PROJ_TILE = 512
GLA_CHUNK = 128
GLA_FAST_MAX_DECAY = 60.0
COMBINE_TILE = 512
NEG_BIG = -1e30
VMEM_LIMIT = 48 * 1024 * 1024

BF16 = jnp.bfloat16
F32 = jnp.float32


def _dot(a, b):
    return jnp.dot(a, b, preferred_element_type=F32)


def _dot_nt(a, b):
    return lax.dot_general(a, b, (((1,), (1,)), ((), ())), preferred_element_type=F32)


def _dot_tn(a, b):
    return lax.dot_general(a, b, (((0,), (0,)), ((), ())), preferred_element_type=F32)


def _split_bf16(x):
    hi = x.astype(BF16)
    lo = (x - hi.astype(F32)).astype(BF16)
    return hi, lo


TOKEN_TILE_ROWS = D_MODEL // LANES


def _token_tile_spec(tokens, index_map=lambda i: (i, 0)):
    return pl.BlockSpec((tokens * TOKEN_TILE_ROWS, LANES), index_map)


def _load_token_tiles(ref, tokens):
    return jnp.concatenate([ref[pl.ds(c, tokens, stride=TOKEN_TILE_ROWS), :] for c in range(TOKEN_TILE_ROWS)],
                           axis=1)


def _store_token_tiles(ref, x):
    for c in range(TOKEN_TILE_ROWS):
        ref[pl.ds(c, x.shape[0], stride=TOKEN_TILE_ROWS), :] = x[:, c * LANES:(c + 1) * LANES]


def _in_proj_kernel(x_ref, w_ref, wgf_ref, bgf_ref, wgb_ref, bgb_ref, cos_ref, sin_ref,
                    aq_ref, ak_ref, av_ref, gq_ref, gk_ref, gv_ref, gr_ref, lgf_ref, lgb_ref):
    xb = x_ref[...].astype(BF16)
    cos = cos_ref[...]
    sin = sin_ref[...]
    lane = lax.broadcasted_iota(jnp.int32, (x_ref.shape[0], LANES), 1)
    first_half = (lane % ATT_HEAD_DIM) < (ATT_HEAD_DIM // 2)

    def rope(t):
        swapped = jnp.where(first_half, pltpu.roll(t, LANES - ATT_HEAD_DIM // 2, 1),
                            pltpu.roll(t, ATT_HEAD_DIM // 2, 1))
        return t * cos + swapped * sin

    def proj(lo, width):
        return _dot(xb, w_ref[:, lo:lo + width])

    q_scale = ATT_HEAD_DIM ** -0.5
    t = proj(OFF_AQ, ATT_Q_W)
    for c in range(ATT_Q_W // LANES):
        cs = slice(c * LANES, (c + 1) * LANES)
        aq_ref[:, cs] = (rope(t[:, cs]) * q_scale).astype(aq_ref.dtype)
    kv = proj(OFF_AK, 2 * ATT_KV_W)
    ak_ref[...] = rope(kv[:, :ATT_KV_W]).astype(ak_ref.dtype)
    av_ref[...] = kv[:, ATT_KV_W:].astype(av_ref.dtype)
    gq_ref[...] = proj(OFF_GQ, GLA_K_W) * (GLA_DK ** -0.5)
    gk_ref[...] = proj(OFF_GK, GLA_K_W)
    gv_ref[...] = proj(OFF_GV, GLA_V_W).astype(gv_ref.dtype)
    grz = proj(OFF_GR, GLA_V_W + 2 * GLA_GATE_RANK)
    gr_ref[...] = grz[:, :GLA_V_W].astype(gr_ref.dtype)
    z = grz[:, GLA_V_W:]

    def log_decay(zr, wg_ref, bg_ref):
        pre = _dot(zr.astype(BF16), wg_ref[...]) + bg_ref[...]
        log_sig = jnp.minimum(pre, 0.0) - jnp.log(1.0 + jnp.exp(-jnp.abs(pre)))
        return log_sig * (1.0 / GLA_TAU)

    lgf_ref[...] = log_decay(z[:, :GLA_GATE_RANK], wgf_ref, bgf_ref)
    lgb_ref[...] = log_decay(z[:, GLA_GATE_RANK:], wgb_ref, bgb_ref)


def _in_proj(x2, w_in, wgf, bgf, wgb, bgb, cos_t, sin_t, seq):
    T = x2.shape[0]
    tm = PROJ_TILE
    assert T % tm == 0 and seq % tm == 0
    pos_tiles = seq // tm
    row = lambda w: pl.BlockSpec((tm, w), lambda i: (i, 0))
    full = lambda a: pl.BlockSpec(a.shape, lambda i: (0,) * a.ndim)
    pos = pl.BlockSpec((tm, LANES), lambda i: (i % pos_tiles, 0))
    out_w = [(ATT_Q_W, BF16), (ATT_KV_W, BF16), (ATT_KV_W, BF16), (GLA_K_W, F32), (GLA_K_W, F32),
             (GLA_V_W, BF16), (GLA_V_W, BF16), (GLA_K_W, F32), (GLA_K_W, F32)]
    return pl.pallas_call(
        _in_proj_kernel,
        grid=(T // tm,),
        in_specs=[row(D_MODEL), full(w_in), full(wgf), full(bgf), full(wgb), full(bgb), pos, pos],
        out_specs=[row(w) for w, _ in out_w],
        out_shape=[jax.ShapeDtypeStruct((T, w), dt) for w, dt in out_w],
        compiler_params=pltpu.CompilerParams(dimension_semantics=("arbitrary",), vmem_limit_bytes=VMEM_LIMIT),
        name="in_proj",
    )(x2, w_in, wgf, bgf, wgb, bgb, cos_t, sin_t)


ATT_STEP_BLOCKS = 2


def _attention_kernel(sinks_ref, q_ref, kp_ref, kc_ref, kn_ref, vp_ref, vc_ref, vn_ref, o_ref, *, seq):
    n = pl.program_id(1)
    R = ATT_BLOCK
    D = ATT_HEAD_DIM
    QB = ATT_STEP_BLOCKS
    k_all = jnp.concatenate([kp_ref[...], kc_ref[...], kn_ref[...]], axis=0)
    v_all = jnp.concatenate([vp_ref[...], vc_ref[...], vn_ref[...]], axis=0)
    qi = lax.broadcasted_iota(jnp.int32, (R, 3 * R), 0)
    kj = lax.broadcasted_iota(jnp.int32, (R, 3 * R), 1)
    in_window = jnp.abs(kj - R - qi) <= WINDOW
    assert ATT_KV_HEADS * D == LANES
    v_lane = lax.broadcasted_iota(jnp.int32, (3 * R, LANES), 1)
    tiles = [(j, h) for j in range(QB) for h in range(ATT_HEADS)]

    scores = []
    for j in range(QB):
        key_pos = (n * QB + j) * R - R + kj
        valid = in_window & (key_pos >= 0) & (key_pos < seq)
        kcat = k_all[j * R:(j + 3) * R]
        for h in range(ATT_HEADS):
            kv = h // ATT_GROUP
            s = _dot_nt(q_ref[j * R:(j + 1) * R, h * D:(h + 1) * D], kcat[:, kv * D:(kv + 1) * D])
            scores.append(jnp.where(valid, s, NEG_BIG))
    probs, sink_terms = [], []
    for (j, h), s in zip(tiles, scores):
        sink = sinks_ref[h]
        m = jnp.maximum(jnp.max(s, axis=1, keepdims=True), sink)
        probs.append(jnp.exp(s - m).astype(BF16))
        sink_terms.append(jnp.exp(sink - m))
    v_aug = {}
    for j in range(QB):
        vcat = v_all[j * R:(j + 3) * R]
        for kv in range(ATT_KV_HEADS):
            v_aug[j, kv] = jnp.where((v_lane < D) == (kv == 0), vcat, jnp.ones_like(vcat))
    for (j, h), p, sink_term in zip(tiles, probs, sink_terms):
        kv = h // ATT_GROUP
        res = _dot(p, v_aug[j, kv])
        val = res[:, kv * D:(kv + 1) * D]
        den = res[:, (1 - kv) * D:(1 - kv) * D + 1] + sink_term
        o_ref[j * R:(j + 1) * R, h * D:(h + 1) * D] = (val / den).astype(o_ref.dtype)


def _attention(aq, ak, av, sinks, batch, seq):
    R = ATT_BLOCK
    QB = ATT_STEP_BLOCKS
    nb = seq // R
    assert nb % QB == 0
    steps = nb // QB
    qspec = pl.BlockSpec((QB * R, ATT_Q_W), lambda b, n: (b * steps + n, 0))
    prev = pl.BlockSpec((R, ATT_KV_W), lambda b, n: (b * nb + jnp.maximum(QB * n - 1, 0), 0))
    cur = pl.BlockSpec((QB * R, ATT_KV_W), lambda b, n: (b * steps + n, 0))
    nxt = pl.BlockSpec((R, ATT_KV_W), lambda b, n: (b * nb + jnp.minimum(QB * n + QB, nb - 1), 0))
    return pl.pallas_call(
        functools.partial(_attention_kernel, seq=seq),
        grid=(batch, steps),
        in_specs=[pl.BlockSpec(memory_space=pltpu.SMEM), qspec, prev, cur, nxt, prev, cur, nxt],
        out_specs=qspec,
        out_shape=jax.ShapeDtypeStruct(aq.shape, BF16),
        compiler_params=pltpu.CompilerParams(dimension_semantics=("arbitrary", "arbitrary"),
                                             vmem_limit_bytes=VMEM_LIMIT),
        name="swa_attention",
    )(sinks, aq, ak, ak, ak, av, av, av)


_GLA_HEAD_SLICES = [(slice(h * GLA_DK, (h + 1) * GLA_DK), slice(h * GLA_DV, (h + 1) * GLA_DV)) for h in range(GLA_HEADS)]


def _gla_decays(q_ref, k_ref, lg_ref, reverse):
    C = q_ref.shape[0]
    ri = lax.broadcasted_iota(jnp.int32, (C, C), 0)
    cj = lax.broadcasted_iota(jnp.int32, (C, C), 1)
    pair = (cj >= ri) if reverse else (cj <= ri)
    tri = pair.astype(BF16)
    lg_hi, lg_lo = _split_bf16(lg_ref[...])
    bc = _dot(tri, lg_hi) + _dot(tri, lg_lo)
    end = 0 if reverse else C - 1
    b_end = bc[end:end + 1, :]
    qt = (q_ref[...] * jnp.exp(bc)).astype(BF16)
    k_to_end = (k_ref[...] * jnp.exp(b_end - bc)).astype(BF16)
    return pair, bc, b_end, qt, k_to_end


def _gla_exact_intra(q_ref, k_ref, v_ref, bc, oi_ref, bc_ref, vf_ref, reverse):
    C = q_ref.shape[0]
    q = q_ref[...]
    bc_ref[...] = bc
    vf_ref[...] = v_ref[...].astype(F32)
    oi_ref[...] = jnp.zeros_like(oi_ref)
    rows = lax.broadcasted_iota(jnp.int32, (C, 1), 0)

    def body(j, carry):
        kj = k_ref[pl.ds(j, 1), :]
        bj = bc_ref[pl.ds(j, 1), :]
        vj = vf_ref[pl.ds(j, 1), :]
        live = (rows <= j) if reverse else (rows >= j)
        t = jnp.where(live, q * kj * jnp.exp(jnp.minimum(bc - bj, 0.0)), 0.0)
        for ks, vs in _GLA_HEAD_SLICES:
            a = jnp.sum(t[:, ks], axis=1, keepdims=True)
            oi_ref[:, vs] += a * vj[:, vs]
        return carry

    lax.fori_loop(0, C, body, 0)
    return [oi_ref[:, vs] for _, vs in _GLA_HEAD_SLICES]


GLA_STEP_CHUNKS = 2


def _gla_kernel(qf_ref, kf_ref, vf_ref, lgf_ref, qb_ref, kb_ref, vb_ref, lgb_ref, of_ref, ob_ref,
                stf_ref, stb_ref, oi_ref, bc_ref, vv_ref):
    @pl.when(pl.program_id(1) == 0)
    def _():
        stf_ref[...] = jnp.zeros_like(stf_ref)
        stb_ref[...] = jnp.zeros_like(stb_ref)

    C = GLA_CHUNK
    for part in range(GLA_STEP_CHUNKS):
        rows_f = pl.ds(part * C, C)
        rows_b = pl.ds((GLA_STEP_CHUNKS - 1 - part) * C, C)
        fwd = [r.at[rows_f, :] for r in (qf_ref, kf_ref, vf_ref, lgf_ref, of_ref)]
        bwd = [r.at[rows_b, :] for r in (qb_ref, kb_ref, vb_ref, lgb_ref, ob_ref)]
        _gla_chunk_pair(fwd + [stf_ref, False], bwd + [stb_ref, True], oi_ref, bc_ref, vv_ref)


def _gla_chunk_pair(dir_f, dir_b, oi_ref, bc_ref, vv_ref):
    dirs = [dir_f, dir_b]
    lgf_ref, lgb_ref = dir_f[3], dir_b[3]

    def both(fast):
        dec = [_gla_decays(q_ref, k_ref, lg_ref, rev) for q_ref, k_ref, _, lg_ref, _, _, rev in dirs]
        if fast:
            scores = []
            for (_, k_ref, _, _, _, _, _), (_, bc, _, qt, _) in zip(dirs, dec):
                kt = (k_ref[...] * jnp.exp(-bc)).astype(BF16)
                scores.append([_dot_nt(qt[:, ks], kt[:, ks]) for ks, _ in _GLA_HEAD_SLICES])
        inter, new_state = [], []
        for (_, _, v_ref, _, _, st_ref, _), (_, _, b_end, qt, k_to_end) in zip(dirs, dec):
            st = st_ref[...]
            st_b = st.astype(BF16)
            inter.append([_dot_nt(qt[:, ks], st_b[:, ks]) for ks, _ in _GLA_HEAD_SLICES])
            cols = [_dot_tn(v_ref[:, vs], k_to_end[:, ks]) for ks, vs in _GLA_HEAD_SLICES]
            new_state.append(st * jnp.exp(b_end) + jnp.concatenate(cols, axis=1))
        for d, ((q_ref, k_ref, v_ref, _, o_ref, st_ref, rev), (pair, bc, _, _, _)) in enumerate(zip(dirs, dec)):
            if fast:
                intra = [_dot(jnp.where(pair, a, 0.0).astype(BF16), v_ref[:, vs])
                         for a, (_, vs) in zip(scores[d], _GLA_HEAD_SLICES)]
            else:
                intra = _gla_exact_intra(q_ref, k_ref, v_ref, bc, oi_ref, bc_ref, vv_ref, rev)
            for (_, vs), o_inter, o_intra in zip(_GLA_HEAD_SLICES, inter[d], intra):
                o_ref[:, vs] = o_inter + o_intra
            st_ref[...] = new_state[d]

    chunk_decay = jnp.minimum(jnp.min(jnp.sum(lgf_ref[...], axis=0, keepdims=True)),
                              jnp.min(jnp.sum(lgb_ref[...], axis=0, keepdims=True)))
    fast_ok = chunk_decay >= -GLA_FAST_MAX_DECAY
    pl.when(fast_ok)(lambda: both(True))
    pl.when(jnp.logical_not(fast_ok))(lambda: both(False))


def _gla(gq, gk, gv, lgf, lgb, batch, seq):
    C = GLA_CHUNK
    rows = GLA_STEP_CHUNKS * C
    assert seq % rows == 0
    nc = seq // rows
    T = gq.shape[0]
    fwd = lambda w: pl.BlockSpec((rows, w), lambda b, n: (b * nc + n, 0))
    bwd = lambda w: pl.BlockSpec((rows, w), lambda b, n: (b * nc + nc - 1 - n, 0))
    return pl.pallas_call(
        _gla_kernel,
        grid=(batch, nc),
        in_specs=[fwd(GLA_K_W), fwd(GLA_K_W), fwd(GLA_V_W), fwd(GLA_K_W),
                  bwd(GLA_K_W), bwd(GLA_K_W), bwd(GLA_V_W), bwd(GLA_K_W)],
        out_specs=[fwd(GLA_V_W), bwd(GLA_V_W)],
        out_shape=[jax.ShapeDtypeStruct((T, GLA_V_W), F32)] * 2,
        scratch_shapes=[pltpu.VMEM((GLA_DV, GLA_K_W), F32), pltpu.VMEM((GLA_DV, GLA_K_W), F32),
                        pltpu.VMEM((C, GLA_V_W), F32), pltpu.VMEM((C, GLA_K_W), F32),
                        pltpu.VMEM((C, GLA_V_W), F32)],
        compiler_params=pltpu.CompilerParams(dimension_semantics=("arbitrary", "arbitrary"),
                                             vmem_limit_bytes=VMEM_LIMIT),
        name="gla_bidir",
    )(gq, gk, gv, lgf, gq, gk, gv, lgb)


def _layer_norm(y, g, b):
    yc = y - jnp.mean(y, axis=-1, keepdims=True)
    var = jnp.mean(yc * yc, axis=-1, keepdims=True)
    return yc * lax.rsqrt(var + LN_EPS) * g + b


def _post_mixer_kernel(of_ref, ob_ref, gr_ref, att_ref, x_ref, gng_ref, wo_ref, g1_ref, b1_ref,
                       wr_ref, br_ref, x1_ref, ei_ref, gw_ref):
    tm = x_ref.shape[0]
    mixed = [att_ref[...]]
    for h in range(GLA_HEADS):
        vs = slice(h * GLA_DV, (h + 1) * GLA_DV)
        o = of_ref[:, vs] + ob_ref[:, vs]
        o = o * lax.rsqrt(jnp.mean(o * o, axis=-1, keepdims=True) + LN_EPS) * gng_ref[...]
        r = gr_ref[:, vs].astype(F32)
        mixed.append((o * (r / (1.0 + jnp.exp(-r)))).astype(BF16))
    m = _dot(jnp.concatenate(mixed, axis=1), wo_ref[...])
    x1 = _layer_norm(DEEPNORM_ALPHA * x_ref[...] + m, g1_ref[...], b1_ref[...])
    _store_token_tiles(x1_ref, x1)

    xh, xl = _split_bf16(x1)
    hi_lo = _dot(xh, wr_ref[...])
    logits = hi_lo[:, :N_EXPERTS] + hi_lo[:, N_EXPERTS:] + _dot(xl, wr_ref[:, :N_EXPERTS]) + br_ref[...]
    lane = lax.broadcasted_iota(jnp.int32, (tm, N_EXPERTS), 1).astype(F32)
    out_lane = lax.broadcasted_iota(jnp.int32, (tm, LANES), 1)
    vals = logits
    top_v = []
    e_out = jnp.zeros((tm, LANES), jnp.int32)
    for kk in range(TOP_K):
        mx = jnp.max(vals, axis=1, keepdims=True)
        idx = jnp.min(jnp.where(vals == mx, lane, float(N_EXPERTS)), axis=1, keepdims=True)
        top_v.append(mx)
        e_out = jnp.where(out_lane == kk, idx.astype(jnp.int32), e_out)
        vals = jnp.where(lane == idx, -jnp.inf, vals)
    ex = [jnp.exp(v - top_v[0]) for v in top_v]
    den = ex[0] + ex[1] + ex[2] + ex[3]
    g_out = jnp.zeros((tm, LANES), F32)
    for kk in range(TOP_K):
        g_out = jnp.where(out_lane == kk, ex[kk] / den, g_out)
    ei_ref[...] = e_out
    gw_ref[...] = g_out


def _post_mixer(o_f, o_b, gr, att, x2, gng, w_out, g1, b1, wr_cat, br):
    T = x2.shape[0]
    tm = PROJ_TILE
    row = lambda w: pl.BlockSpec((tm, w), lambda i: (i, 0))
    full = lambda a: pl.BlockSpec(a.shape, lambda i: (0,) * a.ndim)
    return pl.pallas_call(
        _post_mixer_kernel,
        grid=(T // tm,),
        in_specs=[row(GLA_V_W), row(GLA_V_W), row(GLA_V_W), row(ATT_Q_W), row(D_MODEL), full(gng), full(w_out),
                  full(g1), full(b1), full(wr_cat), full(br)],
        out_specs=[_token_tile_spec(tm), row(LANES), row(LANES)],
        out_shape=[jax.ShapeDtypeStruct((T * TOKEN_TILE_ROWS, LANES), F32), jax.ShapeDtypeStruct((T, LANES), jnp.int32),
                   jax.ShapeDtypeStruct((T, LANES), F32)],
        compiler_params=pltpu.CompilerParams(dimension_semantics=("arbitrary",), vmem_limit_bytes=VMEM_LIMIT),
        name="post_mixer_router",
    )(o_f, o_b, gr, att, x2, gng, w_out, g1, b1, wr_cat, br)


MOE_SLOTS = 3
MOE_FF_CHUNK = 256


def _moe_kernel(be_ref, dprev_ref, scur_ref, snext_ref, snext2_ref, x_hbm, wgu_ref, bgu_ref, wd_ref, bd_ref, y_hbm,
                *scratch, plane):
    del be_ref
    R = MOE_BLOCK
    TR = TOKEN_TILE_ROWS
    i = pl.program_id(0)
    xbuf, ybuf = scratch[:MOE_SLOTS], scratch[MOE_SLOTS:2 * MOE_SLOTS]
    gsem, ssem = scratch[2 * MOE_SLOTS:]

    def tile(ref, row):
        return ref.at[pl.ds(pl.multiple_of(row * TR, TR), TR), :]

    def gather_row(idx_ref, r, s):
        tok = idx_ref[0, 0, r] & (plane - 1)
        return pltpu.make_async_copy(tile(x_hbm, tok), tile(xbuf[s], r), gsem.at[s])

    def scatter_row(idx_ref, r, s):
        return pltpu.make_async_copy(tile(ybuf[s], r), tile(y_hbm, idx_ref[0, 0, r]), ssem.at[s])

    def wait_gather(s):
        pltpu.make_async_copy(x_hbm.at[pl.ds(0, R * TR), :], xbuf[s], gsem.at[s]).wait()

    def wait_scatter(s):
        pltpu.make_async_copy(ybuf[s], y_hbm.at[pl.ds(0, R * TR), :], ssem.at[s]).wait()

    @pl.when(i == 0)
    def _():
        def issue(r, carry):
            gather_row(scur_ref, r, 0).start()
            gather_row(snext_ref, r, 1).start()
            return carry

        lax.fori_loop(0, R, issue, 0)
        ybuf[MOE_SLOTS - 1][...] = jnp.zeros((R * TR, LANES), F32)

    def step(slot):
        ahead = (slot + 2) % MOE_SLOTS
        wait_gather(slot)

        @pl.when(i >= MOE_SLOTS - 1)
        def _():
            wait_scatter(slot)

        for r in range(R):
            gather_row(snext2_ref, r, ahead).start()
            scatter_row(dprev_ref, r, ahead).start()

        xb = _load_token_tiles(xbuf[slot], R).astype(BF16)
        acts = []
        for c in range(D_FF // MOE_FF_CHUNK):
            gs = slice(c * MOE_FF_CHUNK, (c + 1) * MOE_FF_CHUNK)
            us = slice(D_FF + c * MOE_FF_CHUNK, D_FF + (c + 1) * MOE_FF_CHUNK)
            gate = jnp.minimum(_dot(xb, wgu_ref[0, :, gs]) + bgu_ref[0, :, gs], SWIGLU_LIMIT)
            up = jnp.clip(_dot(xb, wgu_ref[0, :, us]) + bgu_ref[0, :, us], -SWIGLU_LIMIT, SWIGLU_LIMIT)
            acts.append(((up + 1.0) * gate / (1.0 + jnp.exp(-SWIGLU_ALPHA * gate))).astype(BF16))
        act = jnp.concatenate(acts, axis=1)
        _store_token_tiles(ybuf[slot], _dot(act, wd_ref[0]) + bd_ref[0])

        @pl.when(i == pl.num_programs(0) - 1)
        def _():
            wait_gather((slot + 1) % MOE_SLOTS)
            wait_gather(ahead)
            wait_scatter((slot + 1) % MOE_SLOTS)
            wait_scatter(ahead)

    for slot in range(MOE_SLOTS):
        pl.when(i % MOE_SLOTS == slot)(functools.partial(step, slot))


def _moe(x1t, block_e, row_code, w_gu, b_gu, w_dn, b_dn, plane):
    R = MOE_BLOCK
    TR = TOKEN_TILE_ROWS
    steps = block_e.shape[0]
    smem_blk = lambda off: pl.BlockSpec((1, 1, R), lambda i, be: (i + off, 0, 0), memory_space=pltpu.SMEM)
    grid_spec = pltpu.PrefetchScalarGridSpec(
        num_scalar_prefetch=1,
        grid=(steps,),
        in_specs=[
            smem_blk(0), smem_blk(1), smem_blk(2), smem_blk(3),
            pl.BlockSpec(memory_space=pl.ANY),
            pl.BlockSpec((1, D_MODEL, 2 * D_FF), lambda i, be: (be[i], 0, 0)),
            pl.BlockSpec((1, 1, 2 * D_FF), lambda i, be: (be[i], 0, 0)),
            pl.BlockSpec((1, D_FF, D_MODEL), lambda i, be: (be[i], 0, 0)),
            pl.BlockSpec((1, 1, D_MODEL), lambda i, be: (be[i], 0, 0)),
        ],
        out_specs=pl.BlockSpec(memory_space=pl.ANY),
        scratch_shapes=[pltpu.VMEM((R * TR, LANES), F32)] * (2 * MOE_SLOTS)
        + [pltpu.SemaphoreType.DMA((MOE_SLOTS,)), pltpu.SemaphoreType.DMA((MOE_SLOTS,))],
    )
    return pl.pallas_call(
        functools.partial(_moe_kernel, plane=plane),
        grid_spec=grid_spec,
        out_shape=jax.ShapeDtypeStruct(((TOP_K * plane + 2 * R) * TR, LANES), F32),
        compiler_params=pltpu.CompilerParams(dimension_semantics=("arbitrary",), vmem_limit_bytes=VMEM_LIMIT),
        name="moe_experts",
    )(block_e, row_code, row_code, row_code, row_code, x1t, w_gu, b_gu, w_dn, b_dn)


def _combine_kernel(y0_ref, y1_ref, y2_ref, y3_ref, gw_ref, x1_ref, g2_ref, b2_ref, out_ref):
    tm = out_ref.shape[0]
    gw = gw_ref[...]
    f = gw[:, 0:1] * _load_token_tiles(y0_ref, tm)
    for kk, y_ref in enumerate((y1_ref, y2_ref, y3_ref), start=1):
        f = f + gw[:, kk:kk + 1] * _load_token_tiles(y_ref, tm)
    out_ref[...] = _layer_norm(DEEPNORM_ALPHA * _load_token_tiles(x1_ref, tm) + f, g2_ref[...], b2_ref[...])


def _combine(y4, gw, x1t, g2, b2, plane):
    T = gw.shape[0]
    tm = COMBINE_TILE
    plane_tiles = plane // tm
    row = lambda w: pl.BlockSpec((tm, w), lambda i: (i, 0))
    full = lambda a: pl.BlockSpec(a.shape, lambda i: (0,) * a.ndim)
    slot_plane = lambda kk: _token_tile_spec(tm, lambda i: (kk * plane_tiles + i, 0))
    return pl.pallas_call(
        _combine_kernel,
        grid=(T // tm,),
        in_specs=[slot_plane(0), slot_plane(1), slot_plane(2), slot_plane(3), row(LANES), _token_tile_spec(tm),
                  full(g2), full(b2)],
        out_specs=row(D_MODEL),
        out_shape=jax.ShapeDtypeStruct((T, D_MODEL), F32),
        compiler_params=pltpu.CompilerParams(dimension_semantics=("arbitrary",), vmem_limit_bytes=VMEM_LIMIT),
        name="moe_combine_ln",
    )(y4, y4, y4, y4, gw, x1t, g2, b2)


def _routing_tables(e_idx, plane):
    T = e_idx.shape[0]
    R, E = MOE_BLOCK, N_EXPERTS
    M = T * TOP_K
    m_bits = (M - 1).bit_length()
    assert (E << m_bits) < 2 ** 31
    flat_e = e_idx.reshape(M)
    m_sorted = jnp.sort((flat_e << m_bits) + jnp.arange(M, dtype=jnp.int32)) & ((1 << m_bits) - 1)
    code_sorted = (m_sorted % TOP_K) * plane + m_sorted // TOP_K
    counts = jnp.sum((flat_e[:, None] == jnp.arange(E, dtype=jnp.int32)[None, :]).astype(jnp.int32), axis=0)
    padded = ((counts + R - 1) // R) * R
    ends_pad = jnp.cumsum(padded)
    starts_pad = ends_pad - padded
    starts = jnp.cumsum(counts) - counts
    n_blocks = M // R + E
    block_start = jnp.arange(n_blocks + 1, dtype=jnp.int32) * R
    block_e = jnp.minimum(jnp.sum(block_start[:, None] >= ends_pad[None, :], axis=1), E - 1).astype(jnp.int32)
    be = block_e[:n_blocks]
    in_row = jnp.arange(R, dtype=jnp.int32)[None, :]
    local = (block_start[:n_blocks] - starts_pad[be])[:, None] + in_row
    valid = local < counts[be][:, None]
    src = jnp.minimum(starts[be][:, None] + local, M - 1)
    parity = (jnp.arange(n_blocks + 4, dtype=jnp.int32) % 2)[:, None]
    spare = TOP_K * plane + parity * R + in_row
    body = jnp.where(valid, code_sorted[src], spare[1:n_blocks + 1])
    row_code = jnp.concatenate([spare[:1], body, spare[n_blocks + 1:]], axis=0)
    return row_code.reshape(n_blocks + 4, 1, R), block_e


def _rope_tables(seq):
    half = ATT_HEAD_DIM // 2
    inv_freq = ROPE_THETA ** (-jnp.arange(half, dtype=F32) * 2.0 / ATT_HEAD_DIM)
    ang = jnp.arange(seq, dtype=F32)[:, None] * inv_freq[None, :]
    cos, sin = jnp.cos(ang), jnp.sin(ang)
    reps = LANES // ATT_HEAD_DIM
    cos_t = jnp.tile(jnp.concatenate([cos, cos], axis=1), (1, reps))
    sin_t = jnp.tile(jnp.concatenate([-sin, sin], axis=1), (1, reps))
    return cos_t, sin_t


def _run_trunk(x, p):
    B, S, D = x.shape
    T = B * S
    plane = max(1 << (T - 1).bit_length(), COMBINE_TILE)
    assert T % COMBINE_TILE == 0 and T >= 2 * MOE_BLOCK
    x2 = x.reshape(T, D)
    cos_t, sin_t = _rope_tables(S)
    aq, ak, av, gq, gk, gv, gr, lgf, lgb = _in_proj(x2, p["w_in"], p["wgf"], p["bgf"], p["wgb"], p["bgb"],
                                                    cos_t, sin_t, S)
    att = _attention(aq, ak, av, p["sinks"], B, S)
    o_f, o_b = _gla(gq, gk, gv, lgf, lgb, B, S)
    x1t, e_pad, g_pad = _post_mixer(o_f, o_b, gr, att, x2, p["gng"], p["w_out"], p["g1"], p["b1"],
                                    p["wr_cat"], p["br"])
    row_code, block_e = _routing_tables(e_pad[:, :TOP_K], plane)
    y4 = _moe(x1t, block_e, row_code, p["w_gu"], p["b_gu"], p["w_dn"], p["b_dn"], plane)
    y = _combine(y4, g_pad, x1t, p["g2"], p["b2"], plane)
    return y.reshape(B, S, D)


def kernel(x_prompt, x_sample, w_in, w_gate_f, b_gate_f, w_gate_b, b_gate_b, sinks, gla_norm_g, w_out, ln1_g, ln1_b,
           w_router, b_router, w_gate_up, b_gate_up, w_down, b_down, ln2_g, ln2_b):
    l = 0
    wr = w_router[l]
    wr_hi = wr.astype(BF16)
    p = dict(
        w_in=w_in[l].astype(BF16),
        wgf=w_gate_f[l].astype(BF16), bgf=b_gate_f[l].reshape(1, -1),
        wgb=w_gate_b[l].astype(BF16), bgb=b_gate_b[l].reshape(1, -1),
        sinks=sinks[l], gng=gla_norm_g[l].reshape(1, -1), w_out=w_out[l].astype(BF16),
        g1=ln1_g[l].reshape(1, -1), b1=ln1_b[l].reshape(1, -1),
        wr_cat=jnp.concatenate([wr_hi, (wr - wr_hi.astype(F32)).astype(BF16)], axis=1), br=b_router[l].reshape(1, -1),
        w_gu=w_gate_up[l].astype(BF16), b_gu=b_gate_up[l].reshape(N_EXPERTS, 1, -1),
        w_dn=w_down[l].astype(BF16), b_dn=b_down[l].reshape(N_EXPERTS, 1, -1),
        g2=ln2_g[l].reshape(1, -1), b2=ln2_b[l].reshape(1, -1),
    )
    return (_run_trunk(x_prompt, p), _run_trunk(x_sample, p))
```

```python
import functools

import jax
import jax.numpy as jnp
import numpy as np
from jax import lax
from jax.experimental import pallas as pl
from jax.experimental.pallas import tpu as pltpu

D_MODEL = 1024
DEPTH = 1
ATT_HEADS = 8
ATT_KV_HEADS = 2
ATT_HEAD_DIM = 64
ATT_GROUP = ATT_HEADS // ATT_KV_HEADS
WINDOW = 128
ATT_BLOCK = 128
ROPE_THETA = 10000.0
GLA_HEADS = 4
GLA_DK = 64
GLA_DV = 128
GLA_GATE_RANK = 16
GLA_TAU = 16.0
N_EXPERTS = 32
TOP_K = 4
D_FF = 1024
SWIGLU_LIMIT = 7.0
SWIGLU_ALPHA = 1.702
MOE_BLOCK = 256
LN_EPS = 1e-5
DEEPNORM_ALPHA = (2 * DEPTH) ** 0.25

ATT_Q_W = ATT_HEADS * ATT_HEAD_DIM
ATT_KV_W = ATT_KV_HEADS * ATT_HEAD_DIM
GLA_K_W = GLA_HEADS * GLA_DK
GLA_V_W = GLA_HEADS * GLA_DV
OFF_AQ = 0
OFF_AK = OFF_AQ + ATT_Q_W
OFF_AV = OFF_AK + ATT_KV_W
OFF_GQ = OFF_AV + ATT_KV_W
OFF_GK = OFF_GQ + GLA_K_W
OFF_GV = OFF_GK + GLA_K_W
OFF_GR = OFF_GV + GLA_V_W
OFF_Z = OFF_GR + GLA_V_W
IN_WIDTH = OFF_Z + 2 * GLA_GATE_RANK

LANES = 128
PROJ_TILE = 512
GLA_CHUNK = 128
GLA_FAST_MAX_DECAY = 60.0
COMBINE_TILE = 512
NEG_BIG = -1e30
VMEM_LIMIT = 48 * 1024 * 1024

BF16 = jnp.bfloat16
F32 = jnp.float32


def _dot(a, b):
    return jnp.dot(a, b, preferred_element_type=F32)


def _dot_nt(a, b):
    return lax.dot_general(a, b, (((1,), (1,)), ((), ())), preferred_element_type=F32)


def _dot_tn(a, b):
    return lax.dot_general(a, b, (((0,), (0,)), ((), ())), preferred_element_type=F32)


def _split_bf16(x):
    hi = x.astype(BF16)
    lo = (x - hi.astype(F32)).astype(BF16)
    return hi, lo


TOKEN_TILE_ROWS = D_MODEL // LANES


def _token_tile_spec(tokens, index_map=lambda i: (i, 0)):
    return pl.BlockSpec((tokens * TOKEN_TILE_ROWS, LANES), index_map)


def _load_token_tiles(ref, tokens):
    return jnp.concatenate([ref[pl.ds(c, tokens, stride=TOKEN_TILE_ROWS), :] for c in range(TOKEN_TILE_ROWS)],
                           axis=1)


def _store_token_tiles(ref, x):
    for c in range(TOKEN_TILE_ROWS):
        ref[pl.ds(c, x.shape[0], stride=TOKEN_TILE_ROWS), :] = x[:, c * LANES:(c + 1) * LANES]


def _in_proj_kernel(x_ref, w_ref, wgf_ref, bgf_ref, wgb_ref, bgb_ref, cos_ref, sin_ref,
                    aq_ref, ak_ref, av_ref, gq_ref, gk_ref, gv_ref, gr_ref, lgf_ref, lgb_ref):
    xb = x_ref[...].astype(BF16)
    cos = cos_ref[...]
    sin = sin_ref[...]
    lane = lax.broadcasted_iota(jnp.int32, (x_ref.shape[0], LANES), 1)
    first_half = (lane % ATT_HEAD_DIM) < (ATT_HEAD_DIM // 2)

    def rope(t):
        swapped = jnp.where(first_half, pltpu.roll(t, LANES - ATT_HEAD_DIM // 2, 1),
                            pltpu.roll(t, ATT_HEAD_DIM // 2, 1))
        return t * cos + swapped * sin

    def proj(lo, width):
        return _dot(xb, w_ref[:, lo:lo + width])

    q_scale = ATT_HEAD_DIM ** -0.5
    t = proj(OFF_AQ, ATT_Q_W)
    for c in range(ATT_Q_W // LANES):
        cs = slice(c * LANES, (c + 1) * LANES)
        aq_ref[:, cs] = (rope(t[:, cs]) * q_scale).astype(aq_ref.dtype)
    kv = proj(OFF_AK, 2 * ATT_KV_W)
    ak_ref[...] = rope(kv[:, :ATT_KV_W]).astype(ak_ref.dtype)
    av_ref[...] = kv[:, ATT_KV_W:].astype(av_ref.dtype)
    gq_ref[...] = proj(OFF_GQ, GLA_K_W) * (GLA_DK ** -0.5)
    gk_ref[...] = proj(OFF_GK, GLA_K_W)
    gv_ref[...] = proj(OFF_GV, GLA_V_W).astype(gv_ref.dtype)
    grz = proj(OFF_GR, GLA_V_W + 2 * GLA_GATE_RANK)
    gr_ref[...] = grz[:, :GLA_V_W].astype(gr_ref.dtype)
    z = grz[:, GLA_V_W:]

    def log_decay(zr, wg_ref, bg_ref):
        pre = _dot(zr.astype(BF16), wg_ref[...]) + bg_ref[...]
        log_sig = jnp.minimum(pre, 0.0) - jnp.log(1.0 + jnp.exp(-jnp.abs(pre)))
        return log_sig * (1.0 / GLA_TAU)

    lgf_ref[...] = log_decay(z[:, :GLA_GATE_RANK], wgf_ref, bgf_ref)
    lgb_ref[...] = log_decay(z[:, GLA_GATE_RANK:], wgb_ref, bgb_ref)


def _in_proj(x2, w_in, wgf, bgf, wgb, bgb, cos_t, sin_t, seq):
    T = x2.shape[0]
    tm = PROJ_TILE
    assert T % tm == 0 and seq % tm == 0
    pos_tiles = seq // tm
    row = lambda w: pl.BlockSpec((tm, w), lambda i: (i, 0))
    full = lambda a: pl.BlockSpec(a.shape, lambda i: (0,) * a.ndim)
    pos = pl.BlockSpec((tm, LANES), lambda i: (i % pos_tiles, 0))
    out_w = [(ATT_Q_W, BF16), (ATT_KV_W, BF16), (ATT_KV_W, BF16), (GLA_K_W, F32), (GLA_K_W, F32),
             (GLA_V_W, BF16), (GLA_V_W, BF16), (GLA_K_W, F32), (GLA_K_W, F32)]
    return pl.pallas_call(
        _in_proj_kernel,
        grid=(T // tm,),
        in_specs=[row(D_MODEL), full(w_in), full(wgf), full(bgf), full(wgb), full(bgb), pos, pos],
        out_specs=[row(w) for w, _ in out_w],
        out_shape=[jax.ShapeDtypeStruct((T, w), dt) for w, dt in out_w],
        compiler_params=pltpu.CompilerParams(dimension_semantics=("arbitrary",), vmem_limit_bytes=VMEM_LIMIT),
        name="in_proj",
    )(x2, w_in, wgf, bgf, wgb, bgb, cos_t, sin_t)


ATT_STEP_BLOCKS = 4


def _attention_kernel(sinks_ref, q_ref, kp_ref, kc_ref, kn_ref, vp_ref, vc_ref, vn_ref, o_ref, *, seq):
    n = pl.program_id(1)
    R = ATT_BLOCK
    D = ATT_HEAD_DIM
    QB = ATT_STEP_BLOCKS
    k_all = jnp.concatenate([kp_ref[...], kc_ref[...], kn_ref[...]], axis=0)
    v_all = jnp.concatenate([vp_ref[...], vc_ref[...], vn_ref[...]], axis=0)
    qi = lax.broadcasted_iota(jnp.int32, (R, 3 * R), 0)
    kj = lax.broadcasted_iota(jnp.int32, (R, 3 * R), 1)
    in_window = jnp.abs(kj - R - qi) <= WINDOW
    assert ATT_KV_HEADS * D == LANES
    v_lane = lax.broadcasted_iota(jnp.int32, (3 * R, LANES), 1)
    tiles = [(j, h) for j in range(QB) for h in range(ATT_HEADS)]

    scores = []
    for j in range(QB):
        key_pos = (n * QB + j) * R - R + kj
        valid = in_window & (key_pos >= 0) & (key_pos < seq)
        kcat = k_all[j * R:(j + 3) * R]
        for h in range(ATT_HEADS):
            kv = h // ATT_GROUP
            s = _dot_nt(q_ref[j * R:(j + 1) * R, h * D:(h + 1) * D], kcat[:, kv * D:(kv + 1) * D])
            scores.append(jnp.where(valid, s, NEG_BIG))
    probs, sink_terms = [], []
    for (j, h), s in zip(tiles, scores):
        sink = sinks_ref[h]
        m = jnp.maximum(jnp.max(s, axis=1, keepdims=True), sink)
        probs.append(jnp.exp(s - m).astype(BF16))
        sink_terms.append(jnp.exp(sink - m))
    v_aug = {}
    for j in range(QB):
        vcat = v_all[j * R:(j + 3) * R]
        for kv in range(ATT_KV_HEADS):
            v_aug[j, kv] = jnp.where((v_lane < D) == (kv == 0), vcat, jnp.ones_like(vcat))
    for (j, h), p, sink_term in zip(tiles, probs, sink_terms):
        kv = h // ATT_GROUP
        res = _dot(p, v_aug[j, kv])
        val = res[:, kv * D:(kv + 1) * D]
        den = res[:, (1 - kv) * D:(1 - kv) * D + 1] + sink_term
        o_ref[j * R:(j + 1) * R, h * D:(h + 1) * D] = (val / den).astype(o_ref.dtype)


def _attention(aq, ak, av, sinks, batch, seq):
    R = ATT_BLOCK
    QB = ATT_STEP_BLOCKS
    nb = seq // R
    assert nb % QB == 0
    steps = nb // QB
    qspec = pl.BlockSpec((QB * R, ATT_Q_W), lambda b, n: (b * steps + n, 0))
    prev = pl.BlockSpec((R, ATT_KV_W), lambda b, n: (b * nb + jnp.maximum(QB * n - 1, 0), 0))
    cur = pl.BlockSpec((QB * R, ATT_KV_W), lambda b, n: (b * steps + n, 0))
    nxt = pl.BlockSpec((R, ATT_KV_W), lambda b, n: (b * nb + jnp.minimum(QB * n + QB, nb - 1), 0))
    return pl.pallas_call(
        functools.partial(_attention_kernel, seq=seq),
        grid=(batch, steps),
        in_specs=[pl.BlockSpec(memory_space=pltpu.SMEM), qspec, prev, cur, nxt, prev, cur, nxt],
        out_specs=qspec,
        out_shape=jax.ShapeDtypeStruct(aq.shape, BF16),
        compiler_params=pltpu.CompilerParams(dimension_semantics=("arbitrary", "arbitrary"),
                                             vmem_limit_bytes=VMEM_LIMIT),
        name="swa_attention",
    )(sinks, aq, ak, ak, ak, av, av, av)


_GLA_HEAD_SLICES = [(slice(h * GLA_DK, (h + 1) * GLA_DK), slice(h * GLA_DV, (h + 1) * GLA_DV)) for h in range(GLA_HEADS)]


def _gla_decays(q_ref, k_ref, lg_ref, reverse):
    C = q_ref.shape[0]
    ri = lax.broadcasted_iota(jnp.int32, (C, C), 0)
    cj = lax.broadcasted_iota(jnp.int32, (C, C), 1)
    pair = (cj >= ri) if reverse else (cj <= ri)
    tri = pair.astype(BF16)
    lg_hi, lg_lo = _split_bf16(lg_ref[...])
    bc = _dot(tri, lg_hi) + _dot(tri, lg_lo)
    end = 0 if reverse else C - 1
    b_end = bc[end:end + 1, :]
    qt = (q_ref[...] * jnp.exp(bc)).astype(BF16)
    k_to_end = (k_ref[...] * jnp.exp(b_end - bc)).astype(BF16)
    return pair, bc, b_end, qt, k_to_end


def _gla_exact_intra(q_ref, k_ref, v_ref, bc, oi_ref, bc_ref, vf_ref, reverse):
    C = q_ref.shape[0]
    q = q_ref[...]
    bc_ref[...] = bc
    vf_ref[...] = v_ref[...].astype(F32)
    oi_ref[...] = jnp.zeros_like(oi_ref)
    rows = lax.broadcasted_iota(jnp.int32, (C, 1), 0)

    def body(j, carry):
        kj = k_ref[pl.ds(j, 1), :]
        bj = bc_ref[pl.ds(j, 1), :]
        vj = vf_ref[pl.ds(j, 1), :]
        live = (rows <= j) if reverse else (rows >= j)
        t = jnp.where(live, q * kj * jnp.exp(jnp.minimum(bc - bj, 0.0)), 0.0)
        for ks, vs in _GLA_HEAD_SLICES:
            a = jnp.sum(t[:, ks], axis=1, keepdims=True)
            oi_ref[:, vs] += a * vj[:, vs]
        return carry

    lax.fori_loop(0, C, body, 0)
    return [oi_ref[:, vs] for _, vs in _GLA_HEAD_SLICES]


GLA_STEP_CHUNKS = 4


def _gla_kernel(qf_ref, kf_ref, vf_ref, lgf_ref, qb_ref, kb_ref, vb_ref, lgb_ref, of_ref, ob_ref,
                stf_ref, stb_ref, oi_ref, bc_ref, vv_ref):
    @pl.when(pl.program_id(1) == 0)
    def _():
        stf_ref[...] = jnp.zeros_like(stf_ref)
        stb_ref[...] = jnp.zeros_like(stb_ref)

    C = GLA_CHUNK
    for part in range(GLA_STEP_CHUNKS):
        rows_f = pl.ds(part * C, C)
        rows_b = pl.ds((GLA_STEP_CHUNKS - 1 - part) * C, C)
        fwd = [r.at[rows_f, :] for r in (qf_ref, kf_ref, vf_ref, lgf_ref, of_ref)]
        bwd = [r.at[rows_b, :] for r in (qb_ref, kb_ref, vb_ref, lgb_ref, ob_ref)]
        _gla_chunk_pair(fwd + [stf_ref, False], bwd + [stb_ref, True], oi_ref, bc_ref, vv_ref)


def _gla_chunk_pair(dir_f, dir_b, oi_ref, bc_ref, vv_ref):
    dirs = [dir_f, dir_b]
    lgf_ref, lgb_ref = dir_f[3], dir_b[3]

    def both(fast):
        dec = [_gla_decays(q_ref, k_ref, lg_ref, rev) for q_ref, k_ref, _, lg_ref, _, _, rev in dirs]
        if fast:
            scores = []
            for (_, k_ref, _, _, _, _, _), (_, bc, _, qt, _) in zip(dirs, dec):
                kt = (k_ref[...] * jnp.exp(-bc)).astype(BF16)
                scores.append([_dot_nt(qt[:, ks], kt[:, ks]) for ks, _ in _GLA_HEAD_SLICES])
        inter, new_state = [], []
        for (_, _, v_ref, _, _, st_ref, _), (_, _, b_end, qt, k_to_end) in zip(dirs, dec):
            st = st_ref[...]
            st_b = st.astype(BF16)
            inter.append([_dot_nt(qt[:, ks], st_b[:, ks]) for ks, _ in _GLA_HEAD_SLICES])
            cols = [_dot_tn(v_ref[:, vs], k_to_end[:, ks]) for ks, vs in _GLA_HEAD_SLICES]
            new_state.append(st * jnp.exp(b_end) + jnp.concatenate(cols, axis=1))
        for d, ((q_ref, k_ref, v_ref, _, o_ref, st_ref, rev), (pair, bc, _, _, _)) in enumerate(zip(dirs, dec)):
            if fast:
                intra = [_dot(jnp.where(pair, a, 0.0).astype(BF16), v_ref[:, vs])
                         for a, (_, vs) in zip(scores[d], _GLA_HEAD_SLICES)]
            else:
                intra = _gla_exact_intra(q_ref, k_ref, v_ref, bc, oi_ref, bc_ref, vv_ref, rev)
            for (_, vs), o_inter, o_intra in zip(_GLA_HEAD_SLICES, inter[d], intra):
                o_ref[:, vs] = o_inter + o_intra
            st_ref[...] = new_state[d]

    chunk_decay = jnp.minimum(jnp.min(jnp.sum(lgf_ref[...], axis=0, keepdims=True)),
                              jnp.min(jnp.sum(lgb_ref[...], axis=0, keepdims=True)))
    fast_ok = chunk_decay >= -GLA_FAST_MAX_DECAY
    pl.when(fast_ok)(lambda: both(True))
    pl.when(jnp.logical_not(fast_ok))(lambda: both(False))


def _gla(gq, gk, gv, lgf, lgb, batch, seq):
    C = GLA_CHUNK
    rows = GLA_STEP_CHUNKS * C
    assert seq % rows == 0
    nc = seq // rows
    T = gq.shape[0]
    fwd = lambda w: pl.BlockSpec((rows, w), lambda b, n: (b * nc + n, 0))
    bwd = lambda w: pl.BlockSpec((rows, w), lambda b, n: (b * nc + nc - 1 - n, 0))
    return pl.pallas_call(
        _gla_kernel,
        grid=(batch, nc),
        in_specs=[fwd(GLA_K_W), fwd(GLA_K_W), fwd(GLA_V_W), fwd(GLA_K_W),
                  bwd(GLA_K_W), bwd(GLA_K_W), bwd(GLA_V_W), bwd(GLA_K_W)],
        out_specs=[fwd(GLA_V_W), bwd(GLA_V_W)],
        out_shape=[jax.ShapeDtypeStruct((T, GLA_V_W), F32)] * 2,
        scratch_shapes=[pltpu.VMEM((GLA_DV, GLA_K_W), F32), pltpu.VMEM((GLA_DV, GLA_K_W), F32),
                        pltpu.VMEM((C, GLA_V_W), F32), pltpu.VMEM((C, GLA_K_W), F32),
                        pltpu.VMEM((C, GLA_V_W), F32)],
        compiler_params=pltpu.CompilerParams(dimension_semantics=("arbitrary", "arbitrary"),
                                             vmem_limit_bytes=VMEM_LIMIT),
        name="gla_bidir",
    )(gq, gk, gv, lgf, gq, gk, gv, lgb)


def _layer_norm(y, g, b):
    yc = y - jnp.mean(y, axis=-1, keepdims=True)
    var = jnp.mean(yc * yc, axis=-1, keepdims=True)
    return yc * lax.rsqrt(var + LN_EPS) * g + b


def _post_mixer_kernel(of_ref, ob_ref, gr_ref, att_ref, x_ref, gng_ref, wo_ref, g1_ref, b1_ref,
                       wr_ref, br_ref, x1_ref, ei_ref, gw_ref):
    tm = x_ref.shape[0]
    mixed = [att_ref[...]]
    for h in range(GLA_HEADS):
        vs = slice(h * GLA_DV, (h + 1) * GLA_DV)
        o = of_ref[:, vs] + ob_ref[:, vs]
        o = o * lax.rsqrt(jnp.mean(o * o, axis=-1, keepdims=True) + LN_EPS) * gng_ref[...]
        r = gr_ref[:, vs].astype(F32)
        mixed.append((o * (r / (1.0 + jnp.exp(-r)))).astype(BF16))
    m = _dot(jnp.concatenate(mixed, axis=1), wo_ref[...])
    x1 = _layer_norm(DEEPNORM_ALPHA * x_ref[...] + m, g1_ref[...], b1_ref[...])
    _store_token_tiles(x1_ref, x1)

    xh, xl = _split_bf16(x1)
    hi_lo = _dot(xh, wr_ref[...])
    logits = hi_lo[:, :N_EXPERTS] + hi_lo[:, N_EXPERTS:] + _dot(xl, wr_ref[:, :N_EXPERTS]) + br_ref[...]
    lane = lax.broadcasted_iota(jnp.int32, (tm, N_EXPERTS), 1).astype(F32)
    out_lane = lax.broadcasted_iota(jnp.int32, (tm, LANES), 1)
    vals = logits
    top_v = []
    e_out = jnp.zeros((tm, LANES), jnp.int32)
    for kk in range(TOP_K):
        mx = jnp.max(vals, axis=1, keepdims=True)
        idx = jnp.min(jnp.where(vals == mx, lane, float(N_EXPERTS)), axis=1, keepdims=True)
        top_v.append(mx)
        e_out = jnp.where(out_lane == kk, idx.astype(jnp.int32), e_out)
        vals = jnp.where(lane == idx, -jnp.inf, vals)
    ex = [jnp.exp(v - top_v[0]) for v in top_v]
    den = ex[0] + ex[1] + ex[2] + ex[3]
    g_out = jnp.zeros((tm, LANES), F32)
    for kk in range(TOP_K):
        g_out = jnp.where(out_lane == kk, ex[kk] / den, g_out)
    ei_ref[...] = e_out
    gw_ref[...] = g_out


def _post_mixer(o_f, o_b, gr, att, x2, gng, w_out, g1, b1, wr_cat, br):
    T = x2.shape[0]
    tm = PROJ_TILE
    row = lambda w: pl.BlockSpec((tm, w), lambda i: (i, 0))
    full = lambda a: pl.BlockSpec(a.shape, lambda i: (0,) * a.ndim)
    return pl.pallas_call(
        _post_mixer_kernel,
        grid=(T // tm,),
        in_specs=[row(GLA_V_W), row(GLA_V_W), row(GLA_V_W), row(ATT_Q_W), row(D_MODEL), full(gng), full(w_out),
                  full(g1), full(b1), full(wr_cat), full(br)],
        out_specs=[_token_tile_spec(tm), row(LANES), row(LANES)],
        out_shape=[jax.ShapeDtypeStruct((T * TOKEN_TILE_ROWS, LANES), F32), jax.ShapeDtypeStruct((T, LANES), jnp.int32),
                   jax.ShapeDtypeStruct((T, LANES), F32)],
        compiler_params=pltpu.CompilerParams(dimension_semantics=("arbitrary",), vmem_limit_bytes=VMEM_LIMIT),
        name="post_mixer_router",
    )(o_f, o_b, gr, att, x2, gng, w_out, g1, b1, wr_cat, br)


MOE_SLOTS = 3
MOE_FF_CHUNK = 256


def _moe_kernel(be_ref, dprev_ref, scur_ref, snext_ref, snext2_ref, x_hbm, wgu_ref, bgu_ref, wd_ref, bd_ref, y_hbm,
                *scratch, plane):
    del be_ref
    R = MOE_BLOCK
    TR = TOKEN_TILE_ROWS
    i = pl.program_id(0)
    xbuf, ybuf = scratch[:MOE_SLOTS], scratch[MOE_SLOTS:2 * MOE_SLOTS]
    gsem, ssem = scratch[2 * MOE_SLOTS:]

    def tile(ref, row):
        return ref.at[pl.ds(pl.multiple_of(row * TR, TR), TR), :]

    def gather_row(idx_ref, r, s):
        tok = idx_ref[0, 0, r] & (plane - 1)
        return pltpu.make_async_copy(tile(x_hbm, tok), tile(xbuf[s], r), gsem.at[s])

    def scatter_row(idx_ref, r, s):
        return pltpu.make_async_copy(tile(ybuf[s], r), tile(y_hbm, idx_ref[0, 0, r]), ssem.at[s])

    def wait_gather(s):
        pltpu.make_async_copy(x_hbm.at[pl.ds(0, R * TR), :], xbuf[s], gsem.at[s]).wait()

    def wait_scatter(s):
        pltpu.make_async_copy(ybuf[s], y_hbm.at[pl.ds(0, R * TR), :], ssem.at[s]).wait()

    @pl.when(i == 0)
    def _():
        def issue(r, carry):
            gather_row(scur_ref, r, 0).start()
            gather_row(snext_ref, r, 1).start()
            return carry

        lax.fori_loop(0, R, issue, 0)
        ybuf[MOE_SLOTS - 1][...] = jnp.zeros((R * TR, LANES), F32)

    def step(slot):
        ahead = (slot + 2) % MOE_SLOTS
        wait_gather(slot)

        @pl.when(i >= MOE_SLOTS - 1)
        def _():
            wait_scatter(slot)

        for r in range(R):
            gather_row(snext2_ref, r, ahead).start()
            scatter_row(dprev_ref, r, ahead).start()

        xb = _load_token_tiles(xbuf[slot], R).astype(BF16)
        acts = []
        for c in range(D_FF // MOE_FF_CHUNK):
            gs = slice(c * MOE_FF_CHUNK, (c + 1) * MOE_FF_CHUNK)
            us = slice(D_FF + c * MOE_FF_CHUNK, D_FF + (c + 1) * MOE_FF_CHUNK)
            gate = jnp.minimum(_dot(xb, wgu_ref[0, :, gs]) + bgu_ref[0, :, gs], SWIGLU_LIMIT)
            up = jnp.clip(_dot(xb, wgu_ref[0, :, us]) + bgu_ref[0, :, us], -SWIGLU_LIMIT, SWIGLU_LIMIT)
            acts.append(((up + 1.0) * gate / (1.0 + jnp.exp(-SWIGLU_ALPHA * gate))).astype(BF16))
        act = jnp.concatenate(acts, axis=1)
        _store_token_tiles(ybuf[slot], _dot(act, wd_ref[0]) + bd_ref[0])

        @pl.when(i == pl.num_programs(0) - 1)
        def _():
            wait_gather((slot + 1) % MOE_SLOTS)
            wait_gather(ahead)
            wait_scatter((slot + 1) % MOE_SLOTS)
            wait_scatter(ahead)

    for slot in range(MOE_SLOTS):
        pl.when(i % MOE_SLOTS == slot)(functools.partial(step, slot))


def _moe(x1t, block_e, row_code, w_gu, b_gu, w_dn, b_dn, plane):
    R = MOE_BLOCK
    TR = TOKEN_TILE_ROWS
    steps = block_e.shape[0]
    smem_blk = lambda off: pl.BlockSpec((1, 1, R), lambda i, be: (i + off, 0, 0), memory_space=pltpu.SMEM)
    grid_spec = pltpu.PrefetchScalarGridSpec(
        num_scalar_prefetch=1,
        grid=(steps,),
        in_specs=[
            smem_blk(0), smem_blk(1), smem_blk(2), smem_blk(3),
            pl.BlockSpec(memory_space=pl.ANY),
            pl.BlockSpec((1, D_MODEL, 2 * D_FF), lambda i, be: (be[i], 0, 0)),
            pl.BlockSpec((1, 1, 2 * D_FF), lambda i, be: (be[i], 0, 0)),
            pl.BlockSpec((1, D_FF, D_MODEL), lambda i, be: (be[i], 0, 0)),
            pl.BlockSpec((1, 1, D_MODEL), lambda i, be: (be[i], 0, 0)),
        ],
        out_specs=pl.BlockSpec(memory_space=pl.ANY),
        scratch_shapes=[pltpu.VMEM((R * TR, LANES), F32)] * (2 * MOE_SLOTS)
        + [pltpu.SemaphoreType.DMA((MOE_SLOTS,)), pltpu.SemaphoreType.DMA((MOE_SLOTS,))],
    )
    return pl.pallas_call(
        functools.partial(_moe_kernel, plane=plane),
        grid_spec=grid_spec,
        out_shape=jax.ShapeDtypeStruct(((TOP_K * plane + 2 * R) * TR, LANES), F32),
        compiler_params=pltpu.CompilerParams(dimension_semantics=("arbitrary",), vmem_limit_bytes=VMEM_LIMIT),
        name="moe_experts",
    )(block_e, row_code, row_code, row_code, row_code, x1t, w_gu, b_gu, w_dn, b_dn)


def _combine_kernel(y0_ref, y1_ref, y2_ref, y3_ref, gw_ref, x1_ref, g2_ref, b2_ref, out_ref):
    tm = out_ref.shape[0]
    gw = gw_ref[...]
    f = gw[:, 0:1] * _load_token_tiles(y0_ref, tm)
    for kk, y_ref in enumerate((y1_ref, y2_ref, y3_ref), start=1):
        f = f + gw[:, kk:kk + 1] * _load_token_tiles(y_ref, tm)
    out_ref[...] = _layer_norm(DEEPNORM_ALPHA * _load_token_tiles(x1_ref, tm) + f, g2_ref[...], b2_ref[...])


def _combine(y4, gw, x1t, g2, b2, plane):
    T = gw.shape[0]
    tm = COMBINE_TILE
    plane_tiles = plane // tm
    row = lambda w: pl.BlockSpec((tm, w), lambda i: (i, 0))
    full = lambda a: pl.BlockSpec(a.shape, lambda i: (0,) * a.ndim)
    slot_plane = lambda kk: _token_tile_spec(tm, lambda i: (kk * plane_tiles + i, 0))
    return pl.pallas_call(
        _combine_kernel,
        grid=(T // tm,),
        in_specs=[slot_plane(0), slot_plane(1), slot_plane(2), slot_plane(3), row(LANES), _token_tile_spec(tm),
                  full(g2), full(b2)],
        out_specs=row(D_MODEL),
        out_shape=jax.ShapeDtypeStruct((T, D_MODEL), F32),
        compiler_params=pltpu.CompilerParams(dimension_semantics=("arbitrary",), vmem_limit_bytes=VMEM_LIMIT),
        name="moe_combine_ln",
    )(y4, y4, y4, y4, gw, x1t, g2, b2)


def _routing_tables(e_idx, plane):
    T = e_idx.shape[0]
    R, E = MOE_BLOCK, N_EXPERTS
    M = T * TOP_K
    m_bits = (M - 1).bit_length()
    assert (E << m_bits) < 2 ** 31
    flat_e = e_idx.reshape(M)
    m_sorted = jnp.sort((flat_e << m_bits) + jnp.arange(M, dtype=jnp.int32)) & ((1 << m_bits) - 1)
    code_sorted = (m_sorted % TOP_K) * plane + m_sorted // TOP_K
    counts = jnp.sum((flat_e[:, None] == jnp.arange(E, dtype=jnp.int32)[None, :]).astype(jnp.int32), axis=0)
    padded = ((counts + R - 1) // R) * R
    ends_pad = jnp.cumsum(padded)
    starts_pad = ends_pad - padded
    starts = jnp.cumsum(counts) - counts
    n_blocks = M // R + E
    block_start = jnp.arange(n_blocks + 1, dtype=jnp.int32) * R
    block_e = jnp.minimum(jnp.sum(block_start[:, None] >= ends_pad[None, :], axis=1), E - 1).astype(jnp.int32)
    be = block_e[:n_blocks]
    in_row = jnp.arange(R, dtype=jnp.int32)[None, :]
    local = (block_start[:n_blocks] - starts_pad[be])[:, None] + in_row
    valid = local < counts[be][:, None]
    src = jnp.minimum(starts[be][:, None] + local, M - 1)
    parity = (jnp.arange(n_blocks + 4, dtype=jnp.int32) % 2)[:, None]
    spare = TOP_K * plane + parity * R + in_row
    body = jnp.where(valid, code_sorted[src], spare[1:n_blocks + 1])
    row_code = jnp.concatenate([spare[:1], body, spare[n_blocks + 1:]], axis=0)
    return row_code.reshape(n_blocks + 4, 1, R), block_e


def _rope_tables(seq):
    half = ATT_HEAD_DIM // 2
    inv_freq = ROPE_THETA ** (-jnp.arange(half, dtype=F32) * 2.0 / ATT_HEAD_DIM)
    ang = jnp.arange(seq, dtype=F32)[:, None] * inv_freq[None, :]
    cos, sin = jnp.cos(ang), jnp.sin(ang)
    reps = LANES // ATT_HEAD_DIM
    cos_t = jnp.tile(jnp.concatenate([cos, cos], axis=1), (1, reps))
    sin_t = jnp.tile(jnp.concatenate([-sin, sin], axis=1), (1, reps))
    return cos_t, sin_t


def _run_trunk(x, p):
    B, S, D = x.shape
    T = B * S
    plane = max(1 << (T - 1).bit_length(), COMBINE_TILE)
    assert T % COMBINE_TILE == 0 and T >= 2 * MOE_BLOCK
    x2 = x.reshape(T, D)
    cos_t, sin_t = _rope_tables(S)
    aq, ak, av, gq, gk, gv, gr, lgf, lgb = _in_proj(x2, p["w_in"], p["wgf"], p["bgf"], p["wgb"], p["bgb"],
                                                    cos_t, sin_t, S)
    att = _attention(aq, ak, av, p["sinks"], B, S)
    o_f, o_b = _gla(gq, gk, gv, lgf, lgb, B, S)
    x1t, e_pad, g_pad = _post_mixer(o_f, o_b, gr, att, x2, p["gng"], p["w_out"], p["g1"], p["b1"],
                                    p["wr_cat"], p["br"])
    row_code, block_e = _routing_tables(e_pad[:, :TOP_K], plane)
    y4 = _moe(x1t, block_e, row_code, p["w_gu"], p["b_gu"], p["w_dn"], p["b_dn"], plane)
    y = _combine(y4, g_pad, x1t, p["g2"], p["b2"], plane)
    return y.reshape(B, S, D)


def kernel(x_prompt, x_sample, w_in, w_gate_f, b_gate_f, w_gate_b, b_gate_b, sinks, gla_norm_g, w_out, ln1_g, ln1_b,
           w_router, b_router, w_gate_up, b_gate_up, w_down, b_down, ln2_g, ln2_b):
    l = 0
    wr = w_router[l]
    wr_hi = wr.astype(BF16)
    p = dict(
        w_in=w_in[l].astype(BF16),
        wgf=w_gate_f[l].astype(BF16), bgf=b_gate_f[l].reshape(1, -1),
        wgb=w_gate_b[l].astype(BF16), bgb=b_gate_b[l].reshape(1, -1),
        sinks=sinks[l], gng=gla_norm_g[l].reshape(1, -1), w_out=w_out[l].astype(BF16),
        g1=ln1_g[l].reshape(1, -1), b1=ln1_b[l].reshape(1, -1),
        wr_cat=jnp.concatenate([wr_hi, (wr - wr_hi.astype(F32)).astype(BF16)], axis=1), br=b_router[l].reshape(1, -1),
        w_gu=w_gate_up[l].astype(BF16), b_gu=b_gate_up[l].reshape(N_EXPERTS, 1, -1),
        w_dn=w_down[l].astype(BF16), b_dn=b_down[l].reshape(N_EXPERTS, 1, -1),
        g2=ln2_g[l].reshape(1, -1), b2=ln2_b[l].reshape(1, -1),
    )
    return (_run_trunk(x_prompt, p), _run_trunk(x_sample, p))
```

```python
import functools

import jax
import jax.numpy as jnp
import numpy as np
from jax import lax
from jax.experimental import pallas as pl
from jax.experimental.pallas import tpu as pltpu

D_MODEL = 1024
DEPTH = 1
ATT_HEADS = 8
ATT_KV_HEADS = 2
ATT_HEAD_DIM = 64
ATT_GROUP = ATT_HEADS // ATT_KV_HEADS
WINDOW = 128
ATT_BLOCK = 128
ROPE_THETA = 10000.0
GLA_HEADS = 4
GLA_DK = 64
GLA_DV = 128
GLA_GATE_RANK = 16
GLA_TAU = 16.0
N_EXPERTS = 32
TOP_K = 4
D_FF = 1024
SWIGLU_LIMIT = 7.0
SWIGLU_ALPHA = 1.702
MOE_BLOCK = 256
LN_EPS = 1e-5
DEEPNORM_ALPHA = (2 * DEPTH) ** 0.25

ATT_Q_W = ATT_HEADS * ATT_HEAD_DIM
ATT_KV_W = ATT_KV_HEADS * ATT_HEAD_DIM
GLA_K_W = GLA_HEADS * GLA_DK
GLA_V_W = GLA_HEADS * GLA_DV
OFF_AQ = 0
OFF_AK = OFF_AQ + ATT_Q_W
OFF_AV = OFF_AK + ATT_KV_W
OFF_GQ = OFF_AV + ATT_KV_W
OFF_GK = OFF_GQ + GLA_K_W
OFF_GV = OFF_GK + GLA_K_W
OFF_GR = OFF_GV + GLA_V_W
OFF_Z = OFF_GR + GLA_V_W
IN_WIDTH = OFF_Z + 2 * GLA_GATE_RANK

LANES = 128
PROJ_TILE = 1024
GLA_CHUNK = 128
GLA_FAST_MAX_DECAY = 60.0
COMBINE_TILE = 512
NEG_BIG = -1e30
VMEM_LIMIT = 48 * 1024 * 1024

BF16 = jnp.bfloat16
F32 = jnp.float32


def _dot(a, b):
    return jnp.dot(a, b, preferred_element_type=F32)


def _dot_nt(a, b):
    return lax.dot_general(a, b, (((1,), (1,)), ((), ())), preferred_element_type=F32)


def _dot_tn(a, b):
    return lax.dot_general(a, b, (((0,), (0,)), ((), ())), preferred_element_type=F32)


def _split_bf16(x):
    hi = x.astype(BF16)
    lo = (x - hi.astype(F32)).astype(BF16)
    return hi, lo


TOKEN_TILE_ROWS = D_MODEL // LANES


def _token_tile_spec(tokens, index_map=lambda i: (i, 0)):
    return pl.BlockSpec((tokens * TOKEN_TILE_ROWS, LANES), index_map)


def _load_token_tiles(ref, tokens):
    return jnp.concatenate([ref[pl.ds(c, tokens, stride=TOKEN_TILE_ROWS), :] for c in range(TOKEN_TILE_ROWS)],
                           axis=1)


def _store_token_tiles(ref, x):
    for c in range(TOKEN_TILE_ROWS):
        ref[pl.ds(c, x.shape[0], stride=TOKEN_TILE_ROWS), :] = x[:, c * LANES:(c + 1) * LANES]


def _in_proj_kernel(x_ref, w_ref, wgf_ref, bgf_ref, wgb_ref, bgb_ref, cos_ref, sin_ref,
                    aq_ref, ak_ref, av_ref, gq_ref, gk_ref, gv_ref, gr_ref, lgf_ref, lgb_ref):
    xb = x_ref[...].astype(BF16)
    cos = cos_ref[...]
    sin = sin_ref[...]
    lane = lax.broadcasted_iota(jnp.int32, (x_ref.shape[0], LANES), 1)
    first_half = (lane % ATT_HEAD_DIM) < (ATT_HEAD_DIM // 2)

    def rope(t):
        swapped = jnp.where(first_half, pltpu.roll(t, LANES - ATT_HEAD_DIM // 2, 1),
                            pltpu.roll(t, ATT_HEAD_DIM // 2, 1))
        return t * cos + swapped * sin

    def proj(lo, width):
        return _dot(xb, w_ref[:, lo:lo + width])

    q_scale = ATT_HEAD_DIM ** -0.5
    t = proj(OFF_AQ, ATT_Q_W)
    for c in range(ATT_Q_W // LANES):
        cs = slice(c * LANES, (c + 1) * LANES)
        aq_ref[:, cs] = (rope(t[:, cs]) * q_scale).astype(aq_ref.dtype)
    kv = proj(OFF_AK, 2 * ATT_KV_W)
    ak_ref[...] = rope(kv[:, :ATT_KV_W]).astype(ak_ref.dtype)
    av_ref[...] = kv[:, ATT_KV_W:].astype(av_ref.dtype)
    gq_ref[...] = proj(OFF_GQ, GLA_K_W) * (GLA_DK ** -0.5)
    gk_ref[...] = proj(OFF_GK, GLA_K_W)
    gv_ref[...] = proj(OFF_GV, GLA_V_W).astype(gv_ref.dtype)
    grz = proj(OFF_GR, GLA_V_W + 2 * GLA_GATE_RANK)
    gr_ref[...] = grz[:, :GLA_V_W].astype(gr_ref.dtype)
    z = grz[:, GLA_V_W:]

    def log_decay(zr, wg_ref, bg_ref):
        pre = _dot(zr.astype(BF16), wg_ref[...]) + bg_ref[...]
        log_sig = jnp.minimum(pre, 0.0) - jnp.log(1.0 + jnp.exp(-jnp.abs(pre)))
        return log_sig * (1.0 / GLA_TAU)

    lgf_ref[...] = log_decay(z[:, :GLA_GATE_RANK], wgf_ref, bgf_ref)
    lgb_ref[...] = log_decay(z[:, GLA_GATE_RANK:], wgb_ref, bgb_ref)


def _in_proj(x2, w_in, wgf, bgf, wgb, bgb, cos_t, sin_t, seq):
    T = x2.shape[0]
    tm = PROJ_TILE
    assert T % tm == 0 and seq % tm == 0
    pos_tiles = seq // tm
    row = lambda w: pl.BlockSpec((tm, w), lambda i: (i, 0))
    full = lambda a: pl.BlockSpec(a.shape, lambda i: (0,) * a.ndim)
    pos = pl.BlockSpec((tm, LANES), lambda i: (i % pos_tiles, 0))
    out_w = [(ATT_Q_W, BF16), (ATT_KV_W, BF16), (ATT_KV_W, BF16), (GLA_K_W, F32), (GLA_K_W, F32),
             (GLA_V_W, BF16), (GLA_V_W, BF16), (GLA_K_W, F32), (GLA_K_W, F32)]
    return pl.pallas_call(
        _in_proj_kernel,
        grid=(T // tm,),
        in_specs=[row(D_MODEL), full(w_in), full(wgf), full(bgf), full(wgb), full(bgb), pos, pos],
        out_specs=[row(w) for w, _ in out_w],
        out_shape=[jax.ShapeDtypeStruct((T, w), dt) for w, dt in out_w],
        compiler_params=pltpu.CompilerParams(dimension_semantics=("arbitrary",), vmem_limit_bytes=VMEM_LIMIT),
        name="in_proj",
    )(x2, w_in, wgf, bgf, wgb, bgb, cos_t, sin_t)


ATT_STEP_BLOCKS = 4


def _attention_kernel(sinks_ref, q_ref, kp_ref, kc_ref, kn_ref, vp_ref, vc_ref, vn_ref, o_ref, *, seq):
    n = pl.program_id(1)
    R = ATT_BLOCK
    D = ATT_HEAD_DIM
    QB = ATT_STEP_BLOCKS
    k_all = jnp.concatenate([kp_ref[...], kc_ref[...], kn_ref[...]], axis=0)
    v_all = jnp.concatenate([vp_ref[...], vc_ref[...], vn_ref[...]], axis=0)
    qi = lax.broadcasted_iota(jnp.int32, (R, 3 * R), 0)
    kj = lax.broadcasted_iota(jnp.int32, (R, 3 * R), 1)
    in_window = jnp.abs(kj - R - qi) <= WINDOW
    assert ATT_KV_HEADS * D == LANES
    v_lane = lax.broadcasted_iota(jnp.int32, (3 * R, LANES), 1)
    tiles = [(j, h) for j in range(QB) for h in range(ATT_HEADS)]

    scores = []
    for j in range(QB):
        key_pos = (n * QB + j) * R - R + kj
        valid = in_window & (key_pos >= 0) & (key_pos < seq)
        kcat = k_all[j * R:(j + 3) * R]
        for h in range(ATT_HEADS):
            kv = h // ATT_GROUP
            s = _dot_nt(q_ref[j * R:(j + 1) * R, h * D:(h + 1) * D], kcat[:, kv * D:(kv + 1) * D])
            scores.append(jnp.where(valid, s, NEG_BIG))
    probs, sink_terms = [], []
    for (j, h), s in zip(tiles, scores):
        sink = sinks_ref[h]
        m = jnp.maximum(jnp.max(s, axis=1, keepdims=True), sink)
        probs.append(jnp.exp(s - m).astype(BF16))
        sink_terms.append(jnp.exp(sink - m))
    v_aug = {}
    for j in range(QB):
        vcat = v_all[j * R:(j + 3) * R]
        for kv in range(ATT_KV_HEADS):
            v_aug[j, kv] = jnp.where((v_lane < D) == (kv == 0), vcat, jnp.ones_like(vcat))
    for (j, h), p, sink_term in zip(tiles, probs, sink_terms):
        kv = h // ATT_GROUP
        res = _dot(p, v_aug[j, kv])
        val = res[:, kv * D:(kv + 1) * D]
        den = res[:, (1 - kv) * D:(1 - kv) * D + 1] + sink_term
        o_ref[j * R:(j + 1) * R, h * D:(h + 1) * D] = (val / den).astype(o_ref.dtype)


def _attention(aq, ak, av, sinks, batch, seq):
    R = ATT_BLOCK
    QB = ATT_STEP_BLOCKS
    nb = seq // R
    assert nb % QB == 0
    steps = nb // QB
    qspec = pl.BlockSpec((QB * R, ATT_Q_W), lambda b, n: (b * steps + n, 0))
    prev = pl.BlockSpec((R, ATT_KV_W), lambda b, n: (b * nb + jnp.maximum(QB * n - 1, 0), 0))
    cur = pl.BlockSpec((QB * R, ATT_KV_W), lambda b, n: (b * steps + n, 0))
    nxt = pl.BlockSpec((R, ATT_KV_W), lambda b, n: (b * nb + jnp.minimum(QB * n + QB, nb - 1), 0))
    return pl.pallas_call(
        functools.partial(_attention_kernel, seq=seq),
        grid=(batch, steps),
        in_specs=[pl.BlockSpec(memory_space=pltpu.SMEM), qspec, prev, cur, nxt, prev, cur, nxt],
        out_specs=qspec,
        out_shape=jax.ShapeDtypeStruct(aq.shape, BF16),
        compiler_params=pltpu.CompilerParams(dimension_semantics=("arbitrary", "arbitrary"),
                                             vmem_limit_bytes=VMEM_LIMIT),
        name="swa_attention",
    )(sinks, aq, ak, ak, ak, av, av, av)


_GLA_HEAD_SLICES = [(slice(h * GLA_DK, (h + 1) * GLA_DK), slice(h * GLA_DV, (h + 1) * GLA_DV)) for h in range(GLA_HEADS)]


def _gla_decays(q_ref, k_ref, lg_ref, reverse):
    C = q_ref.shape[0]
    ri = lax.broadcasted_iota(jnp.int32, (C, C), 0)
    cj = lax.broadcasted_iota(jnp.int32, (C, C), 1)
    pair = (cj >= ri) if reverse else (cj <= ri)
    tri = pair.astype(BF16)
    lg_hi, lg_lo = _split_bf16(lg_ref[...])
    bc = _dot(tri, lg_hi) + _dot(tri, lg_lo)
    end = 0 if reverse else C - 1
    b_end = bc[end:end + 1, :]
    qt = (q_ref[...] * jnp.exp(bc)).astype(BF16)
    k_to_end = (k_ref[...] * jnp.exp(b_end - bc)).astype(BF16)
    return pair, bc, b_end, qt, k_to_end


def _gla_exact_intra(q_ref, k_ref, v_ref, bc, oi_ref, bc_ref, vf_ref, reverse):
    C = q_ref.shape[0]
    q = q_ref[...]
    bc_ref[...] = bc
    vf_ref[...] = v_ref[...].astype(F32)
    oi_ref[...] = jnp.zeros_like(oi_ref)
    rows = lax.broadcasted_iota(jnp.int32, (C, 1), 0)

    def body(j, carry):
        kj = k_ref[pl.ds(j, 1), :]
        bj = bc_ref[pl.ds(j, 1), :]
        vj = vf_ref[pl.ds(j, 1), :]
        live = (rows <= j) if reverse else (rows >= j)
        t = jnp.where(live, q * kj * jnp.exp(jnp.minimum(bc - bj, 0.0)), 0.0)
        for ks, vs in _GLA_HEAD_SLICES:
            a = jnp.sum(t[:, ks], axis=1, keepdims=True)
            oi_ref[:, vs] += a * vj[:, vs]
        return carry

    lax.fori_loop(0, C, body, 0)
    return [oi_ref[:, vs] for _, vs in _GLA_HEAD_SLICES]


GLA_STEP_CHUNKS = 4


def _gla_kernel(qf_ref, kf_ref, vf_ref, lgf_ref, qb_ref, kb_ref, vb_ref, lgb_ref, of_ref, ob_ref,
                stf_ref, stb_ref, oi_ref, bc_ref, vv_ref):
    @pl.when(pl.program_id(1) == 0)
    def _():
        stf_ref[...] = jnp.zeros_like(stf_ref)
        stb_ref[...] = jnp.zeros_like(stb_ref)

    C = GLA_CHUNK
    for part in range(GLA_STEP_CHUNKS):
        rows_f = pl.ds(part * C, C)
        rows_b = pl.ds((GLA_STEP_CHUNKS - 1 - part) * C, C)
        fwd = [r.at[rows_f, :] for r in (qf_ref, kf_ref, vf_ref, lgf_ref, of_ref)]
        bwd = [r.at[rows_b, :] for r in (qb_ref, kb_ref, vb_ref, lgb_ref, ob_ref)]
        _gla_chunk_pair(fwd + [stf_ref, False], bwd + [stb_ref, True], oi_ref, bc_ref, vv_ref)


def _gla_chunk_pair(dir_f, dir_b, oi_ref, bc_ref, vv_ref):
    dirs = [dir_f, dir_b]
    lgf_ref, lgb_ref = dir_f[3], dir_b[3]

    def both(fast):
        dec = [_gla_decays(q_ref, k_ref, lg_ref, rev) for q_ref, k_ref, _, lg_ref, _, _, rev in dirs]
        if fast:
            scores = []
            for (_, k_ref, _, _, _, _, _), (_, bc, _, qt, _) in zip(dirs, dec):
                kt = (k_ref[...] * jnp.exp(-bc)).astype(BF16)
                scores.append([_dot_nt(qt[:, ks], kt[:, ks]) for ks, _ in _GLA_HEAD_SLICES])
        inter, new_state = [], []
        for (_, _, v_ref, _, _, st_ref, _), (_, _, b_end, qt, k_to_end) in zip(dirs, dec):
            st = st_ref[...]
            st_b = st.astype(BF16)
            inter.append([_dot_nt(qt[:, ks], st_b[:, ks]) for ks, _ in _GLA_HEAD_SLICES])
            cols = [_dot_tn(v_ref[:, vs], k_to_end[:, ks]) for ks, vs in _GLA_HEAD_SLICES]
            new_state.append(st * jnp.exp(b_end) + jnp.concatenate(cols, axis=1))
        for d, ((q_ref, k_ref, v_ref, _, o_ref, st_ref, rev), (pair, bc, _, _, _)) in enumerate(zip(dirs, dec)):
            if fast:
                intra = [_dot(jnp.where(pair, a, 0.0).astype(BF16), v_ref[:, vs])
                         for a, (_, vs) in zip(scores[d], _GLA_HEAD_SLICES)]
            else:
                intra = _gla_exact_intra(q_ref, k_ref, v_ref, bc, oi_ref, bc_ref, vv_ref, rev)
            for (_, vs), o_inter, o_intra in zip(_GLA_HEAD_SLICES, inter[d], intra):
                o_ref[:, vs] = o_inter + o_intra
            st_ref[...] = new_state[d]

    chunk_decay = jnp.minimum(jnp.min(jnp.sum(lgf_ref[...], axis=0, keepdims=True)),
                              jnp.min(jnp.sum(lgb_ref[...], axis=0, keepdims=True)))
    fast_ok = chunk_decay >= -GLA_FAST_MAX_DECAY
    pl.when(fast_ok)(lambda: both(True))
    pl.when(jnp.logical_not(fast_ok))(lambda: both(False))


def _gla(gq, gk, gv, lgf, lgb, batch, seq):
    C = GLA_CHUNK
    rows = GLA_STEP_CHUNKS * C
    assert seq % rows == 0
    nc = seq // rows
    T = gq.shape[0]
    fwd = lambda w: pl.BlockSpec((rows, w), lambda b, n: (b * nc + n, 0))
    bwd = lambda w: pl.BlockSpec((rows, w), lambda b, n: (b * nc + nc - 1 - n, 0))
    return pl.pallas_call(
        _gla_kernel,
        grid=(batch, nc),
        in_specs=[fwd(GLA_K_W), fwd(GLA_K_W), fwd(GLA_V_W), fwd(GLA_K_W),
                  bwd(GLA_K_W), bwd(GLA_K_W), bwd(GLA_V_W), bwd(GLA_K_W)],
        out_specs=[fwd(GLA_V_W), bwd(GLA_V_W)],
        out_shape=[jax.ShapeDtypeStruct((T, GLA_V_W), F32)] * 2,
        scratch_shapes=[pltpu.VMEM((GLA_DV, GLA_K_W), F32), pltpu.VMEM((GLA_DV, GLA_K_W), F32),
                        pltpu.VMEM((C, GLA_V_W), F32), pltpu.VMEM((C, GLA_K_W), F32),
                        pltpu.VMEM((C, GLA_V_W), F32)],
        compiler_params=pltpu.CompilerParams(dimension_semantics=("arbitrary", "arbitrary"),
                                             vmem_limit_bytes=VMEM_LIMIT),
        name="gla_bidir",
    )(gq, gk, gv, lgf, gq, gk, gv, lgb)


def _layer_norm(y, g, b):
    yc = y - jnp.mean(y, axis=-1, keepdims=True)
    var = jnp.mean(yc * yc, axis=-1, keepdims=True)
    return yc * lax.rsqrt(var + LN_EPS) * g + b


def _post_mixer_kernel(of_ref, ob_ref, gr_ref, att_ref, x_ref, gng_ref, wo_ref, g1_ref, b1_ref,
                       wr_ref, br_ref, x1_ref, ei_ref, gw_ref):
    tm = x_ref.shape[0]
    mixed = [att_ref[...]]
    for h in range(GLA_HEADS):
        vs = slice(h * GLA_DV, (h + 1) * GLA_DV)
        o = of_ref[:, vs] + ob_ref[:, vs]
        o = o * lax.rsqrt(jnp.mean(o * o, axis=-1, keepdims=True) + LN_EPS) * gng_ref[...]
        r = gr_ref[:, vs].astype(F32)
        mixed.append((o * (r / (1.0 + jnp.exp(-r)))).astype(BF16))
    m = _dot(jnp.concatenate(mixed, axis=1), wo_ref[...])
    x1 = _layer_norm(DEEPNORM_ALPHA * x_ref[...] + m, g1_ref[...], b1_ref[...])
    _store_token_tiles(x1_ref, x1)

    xh, xl = _split_bf16(x1)
    hi_lo = _dot(xh, wr_ref[...])
    logits = hi_lo[:, :N_EXPERTS] + hi_lo[:, N_EXPERTS:] + _dot(xl, wr_ref[:, :N_EXPERTS]) + br_ref[...]
    lane = lax.broadcasted_iota(jnp.int32, (tm, N_EXPERTS), 1).astype(F32)
    out_lane = lax.broadcasted_iota(jnp.int32, (tm, LANES), 1)
    vals = logits
    top_v = []
    e_out = jnp.zeros((tm, LANES), jnp.int32)
    for kk in range(TOP_K):
        mx = jnp.max(vals, axis=1, keepdims=True)
        idx = jnp.min(jnp.where(vals == mx, lane, float(N_EXPERTS)), axis=1, keepdims=True)
        top_v.append(mx)
        e_out = jnp.where(out_lane == kk, idx.astype(jnp.int32), e_out)
        vals = jnp.where(lane == idx, -jnp.inf, vals)
    ex = [jnp.exp(v - top_v[0]) for v in top_v]
    den = ex[0] + ex[1] + ex[2] + ex[3]
    g_out = jnp.zeros((tm, LANES), F32)
    for kk in range(TOP_K):
        g_out = jnp.where(out_lane == kk, ex[kk] / den, g_out)
    ei_ref[...] = e_out
    gw_ref[...] = g_out


def _post_mixer(o_f, o_b, gr, att, x2, gng, w_out, g1, b1, wr_cat, br):
    T = x2.shape[0]
    tm = PROJ_TILE
    row = lambda w: pl.BlockSpec((tm, w), lambda i: (i, 0))
    full = lambda a: pl.BlockSpec(a.shape, lambda i: (0,) * a.ndim)
    return pl.pallas_call(
        _post_mixer_kernel,
        grid=(T // tm,),
        in_specs=[row(GLA_V_W), row(GLA_V_W), row(GLA_V_W), row(ATT_Q_W), row(D_MODEL), full(gng), full(w_out),
                  full(g1), full(b1), full(wr_cat), full(br)],
        out_specs=[_token_tile_spec(tm), row(LANES), row(LANES)],
        out_shape=[jax.ShapeDtypeStruct((T * TOKEN_TILE_ROWS, LANES), F32), jax.ShapeDtypeStruct((T, LANES), jnp.int32),
                   jax.ShapeDtypeStruct((T, LANES), F32)],
        compiler_params=pltpu.CompilerParams(dimension_semantics=("arbitrary",), vmem_limit_bytes=VMEM_LIMIT),
        name="post_mixer_router",
    )(o_f, o_b, gr, att, x2, gng, w_out, g1, b1, wr_cat, br)


MOE_SLOTS = 3
MOE_FF_CHUNK = 512


def _moe_kernel(be_ref, dprev_ref, scur_ref, snext_ref, snext2_ref, x_hbm, wgu_ref, bgu_ref, wd_ref, bd_ref, y_hbm,
                *scratch, plane):
    del be_ref
    R = MOE_BLOCK
    TR = TOKEN_TILE_ROWS
    i = pl.program_id(0)
    xbuf, ybuf = scratch[:MOE_SLOTS], scratch[MOE_SLOTS:2 * MOE_SLOTS]
    gsem, ssem = scratch[2 * MOE_SLOTS:]

    def tile(ref, row):
        return ref.at[pl.ds(pl.multiple_of(row * TR, TR), TR), :]

    def gather_row(idx_ref, r, s):
        tok = idx_ref[0, 0, r] & (plane - 1)
        return pltpu.make_async_copy(tile(x_hbm, tok), tile(xbuf[s], r), gsem.at[s])

    def scatter_row(idx_ref, r, s):
        return pltpu.make_async_copy(tile(ybuf[s], r), tile(y_hbm, idx_ref[0, 0, r]), ssem.at[s])

    def wait_gather(s):
        pltpu.make_async_copy(x_hbm.at[pl.ds(0, R * TR), :], xbuf[s], gsem.at[s]).wait()

    def wait_scatter(s):
        pltpu.make_async_copy(ybuf[s], y_hbm.at[pl.ds(0, R * TR), :], ssem.at[s]).wait()

    @pl.when(i == 0)
    def _():
        def issue(r, carry):
            gather_row(scur_ref, r, 0).start()
            gather_row(snext_ref, r, 1).start()
            return carry

        lax.fori_loop(0, R, issue, 0)
        ybuf[MOE_SLOTS - 1][...] = jnp.zeros((R * TR, LANES), F32)

    def step(slot):
        ahead = (slot + 2) % MOE_SLOTS
        wait_gather(slot)

        @pl.when(i >= MOE_SLOTS - 1)
        def _():
            wait_scatter(slot)

        for r in range(R):
            gather_row(snext2_ref, r, ahead).start()
            scatter_row(dprev_ref, r, ahead).start()

        xb = _load_token_tiles(xbuf[slot], R).astype(BF16)
        acts = []
        for c in range(D_FF // MOE_FF_CHUNK):
            gs = slice(c * MOE_FF_CHUNK, (c + 1) * MOE_FF_CHUNK)
            us = slice(D_FF + c * MOE_FF_CHUNK, D_FF + (c + 1) * MOE_FF_CHUNK)
            gate = jnp.minimum(_dot(xb, wgu_ref[0, :, gs]) + bgu_ref[0, :, gs], SWIGLU_LIMIT)
            up = jnp.clip(_dot(xb, wgu_ref[0, :, us]) + bgu_ref[0, :, us], -SWIGLU_LIMIT, SWIGLU_LIMIT)
            acts.append(((up + 1.0) * gate / (1.0 + jnp.exp(-SWIGLU_ALPHA * gate))).astype(BF16))
        act = jnp.concatenate(acts, axis=1)
        _store_token_tiles(ybuf[slot], _dot(act, wd_ref[0]) + bd_ref[0])

        @pl.when(i == pl.num_programs(0) - 1)
        def _():
            wait_gather((slot + 1) % MOE_SLOTS)
            wait_gather(ahead)
            wait_scatter((slot + 1) % MOE_SLOTS)
            wait_scatter(ahead)

    for slot in range(MOE_SLOTS):
        pl.when(i % MOE_SLOTS == slot)(functools.partial(step, slot))


def _moe(x1t, block_e, row_code, w_gu, b_gu, w_dn, b_dn, plane):
    R = MOE_BLOCK
    TR = TOKEN_TILE_ROWS
    steps = block_e.shape[0]
    smem_blk = lambda off: pl.BlockSpec((1, 1, R), lambda i, be: (i + off, 0, 0), memory_space=pltpu.SMEM)
    grid_spec = pltpu.PrefetchScalarGridSpec(
        num_scalar_prefetch=1,
        grid=(steps,),
        in_specs=[
            smem_blk(0), smem_blk(1), smem_blk(2), smem_blk(3),
            pl.BlockSpec(memory_space=pl.ANY),
            pl.BlockSpec((1, D_MODEL, 2 * D_FF), lambda i, be: (be[i], 0, 0)),
            pl.BlockSpec((1, 1, 2 * D_FF), lambda i, be: (be[i], 0, 0)),
            pl.BlockSpec((1, D_FF, D_MODEL), lambda i, be: (be[i], 0, 0)),
            pl.BlockSpec((1, 1, D_MODEL), lambda i, be: (be[i], 0, 0)),
        ],
        out_specs=pl.BlockSpec(memory_space=pl.ANY),
        scratch_shapes=[pltpu.VMEM((R * TR, LANES), F32)] * (2 * MOE_SLOTS)
        + [pltpu.SemaphoreType.DMA((MOE_SLOTS,)), pltpu.SemaphoreType.DMA((MOE_SLOTS,))],
    )
    return pl.pallas_call(
        functools.partial(_moe_kernel, plane=plane),
        grid_spec=grid_spec,
        out_shape=jax.ShapeDtypeStruct(((TOP_K * plane + 2 * R) * TR, LANES), F32),
        compiler_params=pltpu.CompilerParams(dimension_semantics=("arbitrary",), vmem_limit_bytes=VMEM_LIMIT),
        name="moe_experts",
    )(block_e, row_code, row_code, row_code, row_code, x1t, w_gu, b_gu, w_dn, b_dn)


def _combine_kernel(y0_ref, y1_ref, y2_ref, y3_ref, gw_ref, x1_ref, g2_ref, b2_ref, out_ref):
    tm = out_ref.shape[0]
    gw = gw_ref[...]
    f = gw[:, 0:1] * _load_token_tiles(y0_ref, tm)
    for kk, y_ref in enumerate((y1_ref, y2_ref, y3_ref), start=1):
        f = f + gw[:, kk:kk + 1] * _load_token_tiles(y_ref, tm)
    out_ref[...] = _layer_norm(DEEPNORM_ALPHA * _load_token_tiles(x1_ref, tm) + f, g2_ref[...], b2_ref[...])


def _combine(y4, gw, x1t, g2, b2, plane):
    T = gw.shape[0]
    tm = COMBINE_TILE
    plane_tiles = plane // tm
    row = lambda w: pl.BlockSpec((tm, w), lambda i: (i, 0))
    full = lambda a: pl.BlockSpec(a.shape, lambda i: (0,) * a.ndim)
    slot_plane = lambda kk: _token_tile_spec(tm, lambda i: (kk * plane_tiles + i, 0))
    return pl.pallas_call(
        _combine_kernel,
        grid=(T // tm,),
        in_specs=[slot_plane(0), slot_plane(1), slot_plane(2), slot_plane(3), row(LANES), _token_tile_spec(tm),
                  full(g2), full(b2)],
        out_specs=row(D_MODEL),
        out_shape=jax.ShapeDtypeStruct((T, D_MODEL), F32),
        compiler_params=pltpu.CompilerParams(dimension_semantics=("arbitrary",), vmem_limit_bytes=VMEM_LIMIT),
        name="moe_combine_ln",
    )(y4, y4, y4, y4, gw, x1t, g2, b2)


def _routing_tables(e_idx, plane):
    T = e_idx.shape[0]
    R, E = MOE_BLOCK, N_EXPERTS
    M = T * TOP_K
    m_bits = (M - 1).bit_length()
    assert (E << m_bits) < 2 ** 31
    flat_e = e_idx.reshape(M)
    m_sorted = jnp.sort((flat_e << m_bits) + jnp.arange(M, dtype=jnp.int32)) & ((1 << m_bits) - 1)
    code_sorted = (m_sorted % TOP_K) * plane + m_sorted // TOP_K
    counts = jnp.sum((flat_e[:, None] == jnp.arange(E, dtype=jnp.int32)[None, :]).astype(jnp.int32), axis=0)
    padded = ((counts + R - 1) // R) * R
    ends_pad = jnp.cumsum(padded)
    starts_pad = ends_pad - padded
    starts = jnp.cumsum(counts) - counts
    n_blocks = M // R + E
    block_start = jnp.arange(n_blocks + 1, dtype=jnp.int32) * R
    block_e = jnp.minimum(jnp.sum(block_start[:, None] >= ends_pad[None, :], axis=1), E - 1).astype(jnp.int32)
    be = block_e[:n_blocks]
    in_row = jnp.arange(R, dtype=jnp.int32)[None, :]
    local = (block_start[:n_blocks] - starts_pad[be])[:, None] + in_row
    valid = local < counts[be][:, None]
    src = jnp.minimum(starts[be][:, None] + local, M - 1)
    parity = (jnp.arange(n_blocks + 4, dtype=jnp.int32) % 2)[:, None]
    spare = TOP_K * plane + parity * R + in_row
    body = jnp.where(valid, code_sorted[src], spare[1:n_blocks + 1])
    row_code = jnp.concatenate([spare[:1], body, spare[n_blocks + 1:]], axis=0)
    return row_code.reshape(n_blocks + 4, 1, R), block_e


def _rope_tables(seq):
    half = ATT_HEAD_DIM // 2
    inv_freq = ROPE_THETA ** (-jnp.arange(half, dtype=F32) * 2.0 / ATT_HEAD_DIM)
    ang = jnp.arange(seq, dtype=F32)[:, None] * inv_freq[None, :]
    cos, sin = jnp.cos(ang), jnp.sin(ang)
    reps = LANES // ATT_HEAD_DIM
    cos_t = jnp.tile(jnp.concatenate([cos, cos], axis=1), (1, reps))
    sin_t = jnp.tile(jnp.concatenate([-sin, sin], axis=1), (1, reps))
    return cos_t, sin_t


def _run_trunk(x, p):
    B, S, D = x.shape
    T = B * S
    plane = max(1 << (T - 1).bit_length(), COMBINE_TILE)
    assert T % COMBINE_TILE == 0 and T >= 2 * MOE_BLOCK
    x2 = x.reshape(T, D)
    cos_t, sin_t = _rope_tables(S)
    aq, ak, av, gq, gk, gv, gr, lgf, lgb = _in_proj(x2, p["w_in"], p["wgf"], p["bgf"], p["wgb"], p["bgb"],
                                                    cos_t, sin_t, S)
    att = _attention(aq, ak, av, p["sinks"], B, S)
    o_f, o_b = _gla(gq, gk, gv, lgf, lgb, B, S)
    x1t, e_pad, g_pad = _post_mixer(o_f, o_b, gr, att, x2, p["gng"], p["w_out"], p["g1"], p["b1"],
                                    p["wr_cat"], p["br"])
    row_code, block_e = _routing_tables(e_pad[:, :TOP_K], plane)
    y4 = _moe(x1t, block_e, row_code, p["w_gu"], p["b_gu"], p["w_dn"], p["b_dn"], plane)
    y = _combine(y4, g_pad, x1t, p["g2"], p["b2"], plane)
    return y.reshape(B, S, D)


def kernel(x_prompt, x_sample, w_in, w_gate_f, b_gate_f, w_gate_b, b_gate_b, sinks, gla_norm_g, w_out, ln1_g, ln1_b,
           w_router, b_router, w_gate_up, b_gate_up, w_down, b_down, ln2_g, ln2_b):
    l = 0
    wr = w_router[l]
    wr_hi = wr.astype(BF16)
    p = dict(
        w_in=w_in[l].astype(BF16),
        wgf=w_gate_f[l].astype(BF16), bgf=b_gate_f[l].reshape(1, -1),
        wgb=w_gate_b[l].astype(BF16), bgb=b_gate_b[l].reshape(1, -1),
        sinks=sinks[l], gng=gla_norm_g[l].reshape(1, -1), w_out=w_out[l].astype(BF16),
        g1=ln1_g[l].reshape(1, -1), b1=ln1_b[l].reshape(1, -1),
        wr_cat=jnp.concatenate([wr_hi, (wr - wr_hi.astype(F32)).astype(BF16)], axis=1), br=b_router[l].reshape(1, -1),
        w_gu=w_gate_up[l].astype(BF16), b_gu=b_gate_up[l].reshape(N_EXPERTS, 1, -1),
        w_dn=w_down[l].astype(BF16), b_dn=b_down[l].reshape(N_EXPERTS, 1, -1),
        g2=ln2_g[l].reshape(1, -1), b2=ln2_b[l].reshape(1, -1),
    )
    return (_run_trunk(x_prompt, p), _run_trunk(x_sample, p))
```
